```python
import math
import jax, jax.numpy as jnp
from jax import lax
import numpy as np

D_MODEL = 1024
BATCH = 32
SEQ = 2048
DEPTH = 2
DEC_BATCH = 8
DEC_SEQ = 32
PAST_LEN = 2048

CHUNK = 64
N_RET_HEADS = 4
RET_DK = 128
RET_DV = 256
RET_ROPE_THETA = 10000.0
N_DIFF_HEADS = 8
DIFF_HD = 64
DIFF_ROT = DIFF_HD // 4
ROPE_THETA = 500000.0
Q_BLOCK = 128
D_FF = 2816
CONV_W = 3
EPS = 1e-6

RET_QK = N_RET_HEADS * RET_DK
RET_V = N_RET_HEADS * RET_DV
DIFF_W = N_DIFF_HEADS * 2 * DIFF_HD
SPLITS = (RET_QK, RET_QK, RET_V, RET_V, DIFF_W, DIFF_W, DIFF_W, D_MODEL, D_MODEL)
IN_W = sum(SPLITS)

kernel_name = 'hybrid_retention_diffattn_streaming_step'


def rmsnorm(x, g):
    xf = x.astype(jnp.float32)
    y = xf * lax.rsqrt(jnp.mean(xf * xf, axis=-1, keepdims=True) + EPS)
    return (y * g.astype(jnp.float32)).astype(x.dtype)


def rope(x, pos, rot_dim, theta):
    half = rot_dim // 2
    inv = theta ** (-jnp.arange(half, dtype=jnp.float32) / half)
    ang = pos.astype(jnp.float32)[:, None] * inv[None, :]
    shape = (ang.shape[0],) + (1,) * (x.ndim - 3) + (half,)
    cos = jnp.cos(ang).reshape(shape).astype(x.dtype)
    sin = jnp.sin(ang).reshape(shape).astype(x.dtype)
    xr, xp = x[..., :rot_dim], x[..., rot_dim:]
    x1, x2 = xr[..., :half], xr[..., half:]
    return jnp.concatenate([x1 * cos - x2 * sin, x2 * cos + x1 * sin, xp], axis=-1)


def ret_log_decay():
    return jnp.log1p(-jnp.exp2(-5.0 - jnp.arange(N_RET_HEADS, dtype=jnp.float32)))


def retention_chunk(S, q, k, v):
    L = q.shape[1]
    log_g = ret_log_decay()
    idx = jnp.arange(L, dtype=jnp.float32)
    rel = idx[:, None] - idx[None, :]
    decay = jnp.where(rel >= 0, jnp.exp(log_g[:, None, None] * jnp.maximum(rel, 0.0)), 0.0)
    xi = jnp.exp(log_g[None, :] * (idx[:, None] + 1.0))
    zeta = jnp.exp(log_g[None, :] * (L - 1.0 - idx[:, None]))
    g_len = jnp.exp(log_g * L)
    scores = jnp.einsum('bihd,bjhd->bhij', q, k) * decay.astype(q.dtype)
    o = (jnp.einsum('bhij,bjhe->bihe', scores, v)
         + jnp.einsum('bihd,bhde->bihe', q, S) * xi[None, :, :, None].astype(q.dtype))
    S_new = (S * g_len[None, :, None, None].astype(S.dtype)
             + jnp.einsum('bjhd,bjhe->bhde', k * zeta[None, :, :, None].astype(k.dtype), v))
    return o, S_new.astype(S.dtype)


def retention_prompt(q, k, v):
    B, T = q.shape[:2]
    nc = T // CHUNK
    qc = jnp.moveaxis(q.reshape(B, nc, CHUNK, N_RET_HEADS, RET_DK), 1, 0)
    kc = jnp.moveaxis(k.reshape(B, nc, CHUNK, N_RET_HEADS, RET_DK), 1, 0)
    vc = jnp.moveaxis(v.reshape(B, nc, CHUNK, N_RET_HEADS, RET_DV), 1, 0)
    S0 = jnp.zeros((B, N_RET_HEADS, RET_DK, RET_DV), v.dtype)

    def step(S, xs):
        o, S = retention_chunk(S, xs[0], xs[1], xs[2])
        return S, o

    S_fin, o = lax.scan(step, S0, (qc, kc, vc))
    return jnp.moveaxis(o, 0, 1).reshape(B, T, N_RET_HEADS, RET_DV), S_fin


def diff_attn_block(q, qpos, k, v, kpos, lam):
    s = jnp.einsum('bqhcd,bkhcd->bhcqk', q, k).astype(jnp.float32) * (DIFF_HD ** -0.5)
    visible = (kpos // CHUNK)[None, :] <= (qpos // CHUNK)[:, None]
    s = jnp.where(visible, s, -jnp.inf)
    p = jax.nn.softmax(s, axis=-1)
    a = p[:, :, 0] - lam * p[:, :, 1]
    return jnp.einsum('bhqk,bkhe->bqhe', a.astype(v.dtype), v)


def diff_lambda(p, lam_init):
    f32 = jnp.float32
    return (jnp.exp(jnp.sum(p['lambda_q1'].astype(f32) * p['lambda_k1'].astype(f32)))
            - jnp.exp(jnp.sum(p['lambda_q2'].astype(f32) * p['lambda_k2'].astype(f32))) + lam_init)


def mixer_inputs(h, pos, p):
    B, L = h.shape[:2]
    points = [int(c) for c in np.cumsum(SPLITS)[:-1]]
    q_r, k_r, v_r, g_r, q_d, k_d, v_d, ga, gb = jnp.split(h @ p['w_in'], points, axis=-1)
    q_r = rope(q_r.reshape(B, L, N_RET_HEADS, RET_DK), pos, RET_DK, RET_ROPE_THETA)
    k_r = rope(k_r.reshape(B, L, N_RET_HEADS, RET_DK), pos, RET_DK, RET_ROPE_THETA) * (RET_DK ** -0.5)
    v_r = v_r.reshape(B, L, N_RET_HEADS, RET_DV)
    q_d = rope(rmsnorm(q_d.reshape(B, L, N_DIFF_HEADS, 2, DIFF_HD), p['q_norm_g']), pos, DIFF_ROT, ROPE_THETA)
    k_d = rope(rmsnorm(k_d.reshape(B, L, N_DIFF_HEADS, 2, DIFF_HD), p['k_norm_g']), pos, DIFF_ROT, ROPE_THETA)
    v_d = v_d.reshape(B, L, N_DIFF_HEADS, 2 * DIFF_HD)
    return (q_r, k_r, v_r, g_r), (q_d, k_d, v_d), (ga, gb)


def mixer_output(o_r, g_r, o_d, ga, gb, p, lam_init):
    B, L = o_r.shape[:2]
    ret = rmsnorm(o_r, p['ret_norm_g']).reshape(B, L, RET_V) * jax.nn.silu(g_r)
    dif = (rmsnorm(o_d, p['diff_norm_g']) * (1.0 - lam_init)).reshape(B, L, DIFF_W)
    y = jax.nn.sigmoid(ga) * (ret @ p['w_branch_a']) + jax.nn.sigmoid(gb) * (dif @ p['w_branch_b'])
    return y @ p['w_out']


def conv_glu(h, prev, p):
    u = h @ p['w_up']
    L = u.shape[1]
    pad = jnp.concatenate([prev, u], axis=1)
    c = p['conv_b'] + p['conv_w'][0] * pad[:, 0:L]
    for j in range(1, CONV_W):
        c = c + p['conv_w'][j] * pad[:, j:j + L]
    a, g = jnp.split(c, 2, axis=-1)
    return (jax.nn.silu(g) * a) @ p['w_down'], pad[:, -(CONV_W - 1):]


def layer_prompt(x, pos, p, lam_init):
    B, T = x.shape[:2]
    h = rmsnorm(x, p['norm1_g'])
    (q_r, k_r, v_r, g_r), (q_d, k_d, v_d), (ga, gb) = mixer_inputs(h, pos, p)
    o_r, S = retention_prompt(q_r, k_r, v_r)
    lam = diff_lambda(p, lam_init)
    nb = T // Q_BLOCK
    qb = jnp.moveaxis(q_d.reshape(B, nb, Q_BLOCK, N_DIFF_HEADS, 2, DIFF_HD), 1, 0)
    pb = pos.reshape(nb, Q_BLOCK)
    o_d = lax.map(lambda a: diff_attn_block(a[0], a[1], k_d, v_d, pos, lam), (qb, pb))
    o_d = jnp.moveaxis(o_d, 0, 1).reshape(B, T, N_DIFF_HEADS, 2 * DIFF_HD)
    x = x + mixer_output(o_r, g_r, o_d, ga, gb, p, lam_init)
    f, conv_state = conv_glu(rmsnorm(x, p['norm2_g']), jnp.zeros((B, CONV_W - 1, 2 * D_FF), x.dtype), p)
    return x + f, (k_d.reshape(B, T, N_DIFF_HEADS, 2 * DIFF_HD), v_d, S, conv_state)


def layer_sample(x, pos, ck, cv, s_ret, s_conv, p, lam_init):
    B, L = x.shape[:2]
    past = ck.shape[1]
    h = rmsnorm(x, p['norm1_g'])
    (q_r, k_r, v_r, g_r), (q_d, k_d, v_d), (ga, gb) = mixer_inputs(h, pos, p)
    o_r, S = retention_chunk(s_ret, q_r, k_r, v_r)
    lam = diff_lambda(p, lam_init)
    k_all = jnp.concatenate([ck.reshape(B, past, N_DIFF_HEADS, 2, DIFF_HD), k_d], axis=1)
    v_all = jnp.concatenate([cv, v_d], axis=1)
    kpos = jnp.arange(past + L, dtype=jnp.int32)
    o_d = diff_attn_block(q_d, pos, k_all, v_all, kpos, lam)
    x = x + mixer_output(o_r, g_r, o_d, ga, gb, p, lam_init)
    f, conv_state = conv_glu(rmsnorm(x, p['norm2_g']), s_conv, p)
    return x + f, (k_d.reshape(B, L, N_DIFF_HEADS, 2 * DIFF_HD), v_d, S, conv_state)


def setup_inputs(seed: int = 0) -> dict:
    key = jax.random.key(seed)
    ks = jax.random.split(key, 24)

    def nrm(k, shape, scale):
        return jax.random.normal(k, shape, jnp.float32) * scale

    return {
        'x_prompt': nrm(ks[0], (BATCH, SEQ, D_MODEL), 1.0),
        'x_sample': nrm(ks[1], (DEC_BATCH, DEC_SEQ, D_MODEL), 1.0),
        'cache_k': nrm(ks[2], (DEPTH, DEC_BATCH, PAST_LEN, N_DIFF_HEADS, 2 * DIFF_HD), 1.0),
        'cache_v': nrm(ks[3], (DEPTH, DEC_BATCH, PAST_LEN, N_DIFF_HEADS, 2 * DIFF_HD), 1.0),
        'state_ret': nrm(ks[4], (DEPTH, DEC_BATCH, N_RET_HEADS, RET_DK, RET_DV), 0.5),
        'state_conv': nrm(ks[5], (DEPTH, DEC_BATCH, CONV_W - 1, 2 * D_FF), 1.0),
        'norm1_g': 1.0 + nrm(ks[6], (DEPTH, D_MODEL), 0.05),
        'w_in': nrm(ks[7], (DEPTH, D_MODEL, IN_W), D_MODEL ** -0.5),
        'ret_norm_g': 1.0 + nrm(ks[8], (DEPTH, RET_DV), 0.05),
        'q_norm_g': 1.0 + nrm(ks[9], (DEPTH, DIFF_HD), 0.05),
        'k_norm_g': 1.0 + nrm(ks[10], (DEPTH, DIFF_HD), 0.05),
        'lambda_q1': nrm(ks[11], (DEPTH, DIFF_HD), 0.1),
        'lambda_k1': nrm(ks[12], (DEPTH, DIFF_HD), 0.1),
        'lambda_q2': nrm(ks[13], (DEPTH, DIFF_HD), 0.1),
        'lambda_k2': nrm(ks[14], (DEPTH, DIFF_HD), 0.1),
        'diff_norm_g': 1.0 + nrm(ks[15], (DEPTH, 2 * DIFF_HD), 0.05),
        'w_branch_a': nrm(ks[16], (DEPTH, RET_V, D_MODEL), RET_V ** -0.5),
        'w_branch_b': nrm(ks[17], (DEPTH, DIFF_W, D_MODEL), DIFF_W ** -0.5),
        'w_out': nrm(ks[18], (DEPTH, D_MODEL, D_MODEL), D_MODEL ** -0.5),
        'norm2_g': 1.0 + nrm(ks[19], (DEPTH, D_MODEL), 0.05),
        'w_up': nrm(ks[20], (DEPTH, D_MODEL, 2 * D_FF), D_MODEL ** -0.5),
        'conv_w': nrm(ks[21], (DEPTH, CONV_W, 2 * D_FF), CONV_W ** -0.5),
        'conv_b': nrm(ks[22], (DEPTH, 2 * D_FF), 0.02),
        'w_down': nrm(ks[23], (DEPTH, D_FF, D_MODEL), D_FF ** -0.5),
    }


def reference(x_prompt, x_sample, cache_k, cache_v, state_ret, state_conv, norm1_g, w_in, ret_norm_g,
              q_norm_g, k_norm_g, lambda_q1, lambda_k1, lambda_q2, lambda_k2, diff_norm_g, w_branch_a,
              w_branch_b, w_out, norm2_g, w_up, conv_w, conv_b, w_down):
    past = cache_k.shape[2]
    pos_p = jnp.arange(x_prompt.shape[1], dtype=jnp.int32)
    pos_s = past + jnp.arange(x_sample.shape[1], dtype=jnp.int32)
    yp, ys = x_prompt, x_sample
    kp_l, vp_l, sp_l, cp_l = [], [], [], []
    ks_l, vs_l, ss_l, cs_l = [], [], [], []
    for l in range(DEPTH):
        p = {
            'norm1_g': norm1_g[l], 'w_in': w_in[l], 'ret_norm_g': ret_norm_g[l],
            'q_norm_g': q_norm_g[l], 'k_norm_g': k_norm_g[l],
            'lambda_q1': lambda_q1[l], 'lambda_k1': lambda_k1[l],
            'lambda_q2': lambda_q2[l], 'lambda_k2': lambda_k2[l],
            'diff_norm_g': diff_norm_g[l], 'w_branch_a': w_branch_a[l], 'w_branch_b': w_branch_b[l],
            'w_out': w_out[l], 'norm2_g': norm2_g[l], 'w_up': w_up[l], 'conv_w': conv_w[l],
            'conv_b': conv_b[l], 'w_down': w_down[l],
        }
        lam_init = 0.8 - 0.6 * math.exp(-0.3 * l)
        yp, (kp, vp, sp, cp) = layer_prompt(yp, pos_p, p, lam_init)
        ys, (kn, vn, sn, cn) = layer_sample(ys, pos_s, cache_k[l], cache_v[l], state_ret[l], state_conv[l], p, lam_init)
        kp_l.append(kp); vp_l.append(vp); sp_l.append(sp); cp_l.append(cp)
        ks_l.append(kn); vs_l.append(vn); ss_l.append(sn); cs_l.append(cn)
    return (yp, ys, jnp.stack(kp_l), jnp.stack(vp_l), jnp.stack(sp_l), jnp.stack(cp_l),
            jnp.stack(ks_l), jnp.stack(vs_l), jnp.stack(ss_l), jnp.stack(cs_l))
```

```python
import functools
import math

import jax
import jax.numpy as jnp
import numpy as np
from jax import lax
from jax.experimental import pallas as pl
from jax.experimental.pallas import tpu as pltpu

F32 = jnp.float32
BF16 = jnp.bfloat16

EPS = 1e-6
CHUNK = 64
N_RET_HEADS = 4
RET_DK = 128
RET_DV = 256
RET_ROPE_THETA = 10000.0
N_DIFF_HEADS = 8
DIFF_HD = 64
DIFF_ROT = DIFF_HD // 4
ROPE_THETA = 500000.0
CONV_W = 3

RET_QK = N_RET_HEADS * RET_DK
RET_V = N_RET_HEADS * RET_DV
DIFF_W = N_DIFF_HEADS * 2 * DIFF_HD
OFF_QR = 0
OFF_KR = OFF_QR + RET_QK
OFF_VR = OFF_KR + RET_QK
OFF_GR = OFF_VR + RET_V
OFF_QD = OFF_GR + RET_V
OFF_KD = OFF_QD + DIFF_W
OFF_VD = OFF_KD + DIFF_W
OFF_GA = OFF_VD + DIFF_W

LANES = 128
MXU_N = 256
VMEM_LIMIT = 56 * 1024 * 1024
NEG_BIG = -1e30


def _sigmoid(x):
    return 1.0 / (1.0 + jnp.exp(-x))


def _dot(a, b):
    return jnp.dot(a, b, preferred_element_type=F32)


def _dot_nt(a, b):
    return lax.dot_general(a, b, (((1,), (1,)), ((), ())), preferred_element_type=F32)


def _dot_tn(a, b):
    return lax.dot_general(a, b, (((0,), (0,)), ((), ())), preferred_element_type=F32)


def _inproj_kernel(x_ref, g_ref, w_ref, rcos_ref, rsin_ref, dcos_ref, dsa_ref, dsb_ref, qg_ref, kg_ref,
                   gmat_ref, o16_ref, k32_ref, v32_ref, h_scr, *, tn, d_in):
    j = pl.program_id(1)

    @pl.when(j == 0)
    def _():
        x = x_ref[...]
        ms = jnp.mean(x * x, axis=-1, keepdims=True)
        h_scr[...] = (x * lax.rsqrt(ms + EPS) * g_ref[...]).astype(BF16)

    y = _dot(h_scr[...], w_ref[...])

    def tile_range(off, width):
        return (j >= off // tn) & (j < (off + width) // tn)

    def ret_rope(scale):
        for c in range(tn // LANES):
            ys = y[:, c * LANES:(c + 1) * LANES]
            r = ys * rcos_ref[...] + pltpu.roll(ys, RET_DK // 2, 1) * rsin_ref[...]
            if scale != 1.0:
                r = r * scale
            o16_ref[:, c * LANES:(c + 1) * LANES] = r.astype(BF16)

    def diff_norm_rope(gain_ref, scale, o32_ref):
        for c in range(tn // MXU_N):
            ys = y[:, c * MXU_N:(c + 1) * MXU_N]
            ss = _dot((ys * ys).astype(BF16), gmat_ref[...])
            yn = ys * lax.rsqrt(ss * (1.0 / DIFF_HD) + EPS) * gain_ref[...]
            for d in range(MXU_N // LANES):
                z = yn[:, d * LANES:(d + 1) * LANES]
                r = (z * dcos_ref[...] + pltpu.roll(z, DIFF_ROT // 2, 1) * dsa_ref[...]
                     + pltpu.roll(z, LANES - DIFF_ROT // 2, 1) * dsb_ref[...])
                lo = c * MXU_N + d * LANES
                if o32_ref is not None:
                    o32_ref[:, lo:lo + LANES] = r
                o16_ref[:, lo:lo + LANES] = (r * scale).astype(BF16)

    @pl.when(tile_range(OFF_QR, RET_QK))
    def _():
        ret_rope(1.0)

    @pl.when(tile_range(OFF_KR, RET_QK))
    def _():
        ret_rope(RET_DK ** -0.5)

    @pl.when(tile_range(OFF_VR, RET_V))
    def _():
        o16_ref[...] = y.astype(BF16)

    @pl.when(tile_range(OFF_GR, RET_V))
    def _():
        o16_ref[...] = (y * _sigmoid(y)).astype(BF16)

    @pl.when(tile_range(OFF_QD, DIFF_W))
    def _():
        diff_norm_rope(qg_ref, DIFF_HD ** -0.5, None)

    @pl.when(tile_range(OFF_KD, DIFF_W))
    def _():
        diff_norm_rope(kg_ref, 1.0, k32_ref)

    @pl.when(tile_range(OFF_VD, DIFF_W))
    def _():
        v32_ref[...] = y
        o16_ref[...] = y.astype(BF16)

    @pl.when(j >= OFF_GA // tn)
    def _():
        o16_ref[...] = _sigmoid(y).astype(BF16)


def _inproj(x, g1, w16, tabs, qg, kg, gmat, *, tm, tn):
    n, d_in = x.shape
    in_w = w16.shape[1]
    tt = tabs[0].shape[0]
    nt = tt // tm
    kd0, vd0 = OFF_KD // tn, OFF_VD // tn
    nkv = DIFF_W // tn
    tab_spec = pl.BlockSpec((tm, LANES), lambda i, j: (i % nt, 0))
    return pl.pallas_call(
        functools.partial(_inproj_kernel, tn=tn, d_in=d_in),
        grid=(n // tm, in_w // tn),
        in_specs=[
            pl.BlockSpec((tm, d_in), lambda i, j: (i, 0)),
            pl.BlockSpec((1, d_in), lambda i, j: (0, 0)),
            pl.BlockSpec((d_in, tn), lambda i, j: (0, j)),
            tab_spec, tab_spec, tab_spec, tab_spec, tab_spec,
            pl.BlockSpec((1, MXU_N), lambda i, j: (0, 0)),
            pl.BlockSpec((1, MXU_N), lambda i, j: (0, 0)),
            pl.BlockSpec((MXU_N, MXU_N), lambda i, j: (0, 0)),
        ],
        out_specs=[
            pl.BlockSpec((tm, tn), lambda i, j: (i, j)),
            pl.BlockSpec((tm, tn), lambda i, j: (i, jnp.clip(j - kd0, 0, nkv - 1))),
            pl.BlockSpec((tm, tn), lambda i, j: (i, jnp.clip(j - vd0, 0, nkv - 1))),
        ],
        out_shape=[
            jax.ShapeDtypeStruct((n, in_w), BF16),
            jax.ShapeDtypeStruct((n, DIFF_W), F32),
            jax.ShapeDtypeStruct((n, DIFF_W), F32),
        ],
        scratch_shapes=[pltpu.VMEM((tm, d_in), BF16)],
        compiler_params=pltpu.CompilerParams(
            dimension_semantics=("arbitrary", "arbitrary"), vmem_limit_bytes=VMEM_LIMIT),
        name="inproj",
    )(x, g1, w16, *tabs, qg, kg, gmat)


def _ret_kernel(*refs, c_len, n_chunks, has_state):
    if has_state:
        (q_ref, k_ref, v_ref, g_ref, decay_ref, xi_ref, zeta_ref, glen_ref, gain_ref, s0_ref,
         o_ref, s_ref) = refs
        s = s0_ref[0, 0]
    else:
        (q_ref, k_ref, v_ref, g_ref, decay_ref, xi_ref, zeta_ref, glen_ref, gain_ref,
         o_ref, s_ref) = refs
        s = jnp.zeros((RET_DK, RET_DV), F32)
    decay = decay_ref[0]
    for c in range(n_chunks):
        rows = slice(c * c_len, (c + 1) * c_len)
        q = q_ref[rows, :]
        k = k_ref[rows, :]
        v = v_ref[rows, :]
        sc = _dot_nt(q, k) * decay
        o = _dot(sc.astype(BF16), v) + _dot(q, s.astype(BF16)) * xi_ref[0]
        kz = (k.astype(F32) * zeta_ref[0]).astype(BF16)
        s = s * glen_ref[0] + _dot_tn(kz, v)
        ms = jnp.mean(o * o, axis=-1, keepdims=True)
        on = o * lax.rsqrt(ms + EPS) * gain_ref[...]
        o_ref[rows, :] = (on * g_ref[rows, :].astype(F32)).astype(BF16)
    s_ref[0, 0] = s


def _retention(p16, tables, gain, s0, *, batch, t_len, c_len):
    n = p16.shape[0]
    decay, xi, zeta, glen = tables
    has_state = s0 is not None
    vb0 = OFF_VR // RET_DV
    gb0 = OFF_GR // RET_DV
    kb0 = OFF_KR // RET_DK
    in_specs = [
        pl.BlockSpec((t_len, RET_DK), lambda b, h: (b, h)),
        pl.BlockSpec((t_len, RET_DK), lambda b, h: (b, kb0 + h)),
        pl.BlockSpec((t_len, RET_DV), lambda b, h: (b, vb0 + h)),
        pl.BlockSpec((t_len, RET_DV), lambda b, h: (b, gb0 + h)),
        pl.BlockSpec((1, c_len, c_len), lambda b, h: (h, 0, 0)),
        pl.BlockSpec((1, c_len, RET_DV), lambda b, h: (h, 0, 0)),
        pl.BlockSpec((1, c_len, RET_DK), lambda b, h: (h, 0, 0)),
        pl.BlockSpec((1, 1, RET_DV), lambda b, h: (h, 0, 0)),
        pl.BlockSpec((1, RET_DV), lambda b, h: (0, 0)),
    ]
    args = [p16, p16, p16, p16, decay, xi, zeta, glen, gain]
    if has_state:
        in_specs.append(pl.BlockSpec((1, 1, RET_DK, RET_DV), lambda b, h: (b, h, 0, 0)))
        args.append(s0)
    return pl.pallas_call(
        functools.partial(_ret_kernel, c_len=c_len, n_chunks=t_len // c_len, has_state=has_state),
        grid=(batch, N_RET_HEADS),
        in_specs=in_specs,
        out_specs=[
            pl.BlockSpec((t_len, RET_DV), lambda b, h: (b, h)),
            pl.BlockSpec((1, 1, RET_DK, RET_DV), lambda b, h: (b, h, 0, 0)),
        ],
        out_shape=[
            jax.ShapeDtypeStruct((n, RET_V), BF16),
            jax.ShapeDtypeStruct((batch, N_RET_HEADS, RET_DK, RET_DV), F32),
        ],
        compiler_params=pltpu.CompilerParams(
            dimension_semantics=("arbitrary", "arbitrary"), vmem_limit_bytes=VMEM_LIMIT),
        name="ret",
    )(*args)


def _attn_kernel(q_ref, k_ref, v_ref, lam_ref, gain_ref, o_ref, vt_scr, *, tq, tk, n_kt, causal, n_full,
                 n_valid_last, lam_init):
    qi = pl.program_id(2)

    @pl.when(qi == 0)
    def _():
        for t in range(n_kt):
            vt_scr[t] = v_ref[t * tk:(t + 1) * tk, :].astype(F32).T.astype(BF16)

    q = q_ref[...]
    lane = lax.broadcasted_iota(jnp.int32, q.shape, 1)
    zero = jnp.zeros_like(q)
    qs = (jnp.where(lane < DIFF_HD, q, zero), jnp.where(lane >= DIFF_HD, q, zero))

    def tile(kt, carry, bias):
        off = pl.multiple_of(kt * tk, tk)
        k = k_ref[pl.ds(off, tk), :]
        vt = vt_scr[kt]
        new = []
        for c in range(2):
            m, l, acc = carry[c]
            s = _dot_nt(k, qs[c])
            if bias is not None:
                s = s + bias
            m_new = jnp.maximum(m, jnp.max(s, axis=0, keepdims=True))
            alpha = jnp.exp(m - m_new)
            p = jnp.exp(s - m_new)
            l = alpha * l + jnp.sum(p, axis=0, keepdims=True)
            acc = alpha * acc + _dot(vt, p.astype(BF16))
            new.append((m_new, l, acc))
        return tuple(new)

    init = tuple((jnp.full((1, tq), NEG_BIG, F32), jnp.zeros((1, tq), F32),
                  jnp.zeros((2 * DIFF_HD, tq), F32)) for _ in range(2))
    kidx = lax.broadcasted_iota(jnp.int32, (tk, tq), 0)
    if causal:
        qidx = lax.broadcasted_iota(jnp.int32, (tk, tq), 1)
        visible = (kidx // CHUNK) <= (qidx // CHUNK)
        carry = lax.fori_loop(0, qi, lambda kt, c: tile(kt, c, None), init)
        last = qi
    else:
        visible = kidx < n_valid_last
        carry = lax.fori_loop(0, n_full, lambda kt, c: tile(kt, c, None), init)
        last = n_full
    bias = jnp.where(visible, 0.0, NEG_BIG).astype(F32)
    (m0, l0, a0), (m1, l1, a1) = tile(last, carry, bias)

    lp = lam_ref[...]
    lam = (jnp.exp(jnp.sum(lp[0:1] * lp[1:2], axis=1, keepdims=True))
           - jnp.exp(jnp.sum(lp[2:3] * lp[3:4], axis=1, keepdims=True)) + lam_init)
    o = a0 * (1.0 / l0) - (lam * (1.0 / l1)) * a1
    ms = jnp.mean(o * o, axis=0, keepdims=True)
    on = (o * lax.rsqrt(ms + EPS)).T
    o_ref[...] = (on * gain_ref[...] * (1.0 - lam_init)).astype(BF16)


def _attention(q_arr, k_arr, v_arr, lam_p, gain, *, batch, tq, lq, tk, lk, qcol, kcol, vcol, causal,
               n_full, n_valid_last, lam_init):
    hd2 = 2 * DIFF_HD
    nq = lq // tq
    n_kt = lk // tk
    return pl.pallas_call(
        functools.partial(_attn_kernel, tq=tq, tk=tk, n_kt=n_kt, causal=causal, n_full=n_full,
                          n_valid_last=n_valid_last, lam_init=lam_init),
        grid=(batch, N_DIFF_HEADS, nq),
        in_specs=[
            pl.BlockSpec((tq, hd2), lambda b, h, qi: (b * nq + qi, qcol + h)),
            pl.BlockSpec((lk, hd2), lambda b, h, qi: (b, kcol + h)),
            pl.BlockSpec((lk, hd2), lambda b, h, qi: (b, vcol + h)),
            pl.BlockSpec((4, DIFF_HD), lambda b, h, qi: (0, 0)),
            pl.BlockSpec((1, hd2), lambda b, h, qi: (0, 0)),
        ],
        out_specs=pl.BlockSpec((tq, hd2), lambda b, h, qi: (b * nq + qi, h)),
        out_shape=jax.ShapeDtypeStruct((batch * lq, DIFF_W), BF16),
        scratch_shapes=[pltpu.VMEM((n_kt, hd2, tk), BF16)],
        compiler_params=pltpu.CompilerParams(
            dimension_semantics=("arbitrary", "arbitrary", "arbitrary"), vmem_limit_bytes=VMEM_LIMIT),
        name="attn",
    )(q_arr, k_arr, v_arr, lam_p, gain)


def _merge_kernel(ret_ref, dif_ref, sa_ref, sb_ref, x_ref, wa_ref, wb_ref, wo_ref, g2_ref, x1_ref, h2_ref):
    ya = _dot(ret_ref[...], wa_ref[...])
    yb = _dot(dif_ref[...], wb_ref[...])
    y = sa_ref[...].astype(F32) * ya + sb_ref[...].astype(F32) * yb
    x1 = x_ref[...] + _dot(y.astype(BF16), wo_ref[...])
    x1_ref[...] = x1
    ms = jnp.mean(x1 * x1, axis=-1, keepdims=True)
    h2_ref[...] = (x1 * lax.rsqrt(ms + EPS) * g2_ref[...]).astype(BF16)


def _merge(ret16, dif16, p16, x, wa16, wb16, wo16, g2, *, tm):
    n, d = x.shape
    ga_blk = OFF_GA // d
    full = lambda i: (0, 0)
    return pl.pallas_call(
        _merge_kernel,
        grid=(n // tm,),
        in_specs=[
            pl.BlockSpec((tm, RET_V), lambda i: (i, 0)),
            pl.BlockSpec((tm, DIFF_W), lambda i: (i, 0)),
            pl.BlockSpec((tm, d), lambda i: (i, ga_blk)),
            pl.BlockSpec((tm, d), lambda i: (i, ga_blk + 1)),
            pl.BlockSpec((tm, d), lambda i: (i, 0)),
            pl.BlockSpec((RET_V, d), full),
            pl.BlockSpec((DIFF_W, d), full),
            pl.BlockSpec((d, d), full),
            pl.BlockSpec((1, d), full),
        ],
        out_specs=[pl.BlockSpec((tm, d), lambda i: (i, 0)), pl.BlockSpec((tm, d), lambda i: (i, 0))],
        out_shape=[jax.ShapeDtypeStruct((n, d), F32), jax.ShapeDtypeStruct((n, d), BF16)],
        compiler_params=pltpu.CompilerParams(dimension_semantics=("arbitrary",), vmem_limit_bytes=VMEM_LIMIT),
        name="merge",
    )(ret16, dif16, p16, p16, x, wa16, wb16, wo16, g2)


def _glu_kernel(*refs, tm, n_t, d_ff, cw, has_state):
    if has_state:
        h_ref, x1_ref, wu_ref, wd_ref, cwt_ref, cb_ref, st_ref, y_ref, cs_ref, carry, ua_scr, ug_scr = refs
    else:
        h_ref, x1_ref, wu_ref, wd_ref, cwt_ref, cb_ref, y_ref, cs_ref, carry, ua_scr, ug_scr = refs
    ti = pl.program_id(0) % n_t

    @pl.when(ti == 0)
    def _():
        carry[...] = jnp.zeros_like(carry)
        if has_state:
            carry[8 - (CONV_W - 1):8, :] = st_ref[0]

    h = h_ref[...]
    acc = jnp.zeros(y_ref.shape, F32)
    for c in range(d_ff // cw):
        conv = []
        for scr, base in ((ua_scr, c * cw), (ug_scr, d_ff + c * cw)):
            cols = slice(base, base + cw)
            u = _dot(h, wu_ref[:, cols])
            scr[0:8, :] = carry[:, cols]
            scr[8:8 + tm, :] = u
            carry[8 - (CONV_W - 1):8, cols] = u[tm - (CONV_W - 1):tm, :]
            r = cb_ref[:, cols] + cwt_ref[0:1, cols] * scr[6:6 + tm, :]
            r = r + cwt_ref[1:2, cols] * scr[7:7 + tm, :]
            r = r + cwt_ref[2:3, cols] * u
            conv.append(r)
        a, g = conv
        act = (g * _sigmoid(g) * a).astype(BF16)
        acc = acc + _dot(act, wd_ref[c * cw:(c + 1) * cw, :])
    y_ref[...] = x1_ref[...] + acc

    @pl.when(ti == n_t - 1)
    def _():
        cs_ref[0] = carry[8 - (CONV_W - 1):8, :]


def _glu(h2, x1, wu16, wd16, conv_w, conv_b, state, *, batch, t_len, tm):
    n, d = x1.shape
    d_ff = wd16.shape[0]
    n_t = t_len // tm
    has_state = state is not None
    full = lambda i: (0, 0)
    in_specs = [
        pl.BlockSpec((tm, d), lambda i: (i, 0)),
        pl.BlockSpec((tm, d), lambda i: (i, 0)),
        pl.BlockSpec((d, 2 * d_ff), full),
        pl.BlockSpec((d_ff, d), full),
        pl.BlockSpec((CONV_W, 2 * d_ff), full),
        pl.BlockSpec((1, 2 * d_ff), full),
    ]
    args = [h2, x1, wu16, wd16, conv_w, conv_b]
    if has_state:
        in_specs.append(pl.BlockSpec((1, CONV_W - 1, 2 * d_ff), lambda i: (i // n_t, 0, 0)))
        args.append(state)
    return pl.pallas_call(
        functools.partial(_glu_kernel, tm=tm, n_t=n_t, d_ff=d_ff, cw=MXU_N, has_state=has_state),
        grid=(n // tm,),
        in_specs=in_specs,
        out_specs=[
            pl.BlockSpec((tm, d), lambda i: (i, 0)),
            pl.BlockSpec((1, CONV_W - 1, 2 * d_ff), lambda i: (i // n_t, 0, 0)),
        ],
        out_shape=[
            jax.ShapeDtypeStruct((n, d), F32),
            jax.ShapeDtypeStruct((batch, CONV_W - 1, 2 * d_ff), F32),
        ],
        scratch_shapes=[
            pltpu.VMEM((8, 2 * d_ff), F32),
            pltpu.VMEM((8 + tm, MXU_N), F32),
            pltpu.VMEM((8 + tm, MXU_N), F32),
        ],
        compiler_params=pltpu.CompilerParams(dimension_semantics=("arbitrary",), vmem_limit_bytes=VMEM_LIMIT),
        name="glu",
    )(*args)


def _rope_tables(pos, reps):
    posf = pos.astype(F32)[:, None]
    lane = np.arange(LANES)
    half = RET_DK // 2
    inv = RET_ROPE_THETA ** (-jnp.arange(half, dtype=F32) / half)
    ang = posf * inv[None, :]
    cos, sin = jnp.cos(ang), jnp.sin(ang)
    rcos = jnp.concatenate([cos, cos], axis=1)
    rsin = jnp.concatenate([-sin, sin], axis=1)
    half = DIFF_ROT // 2
    inv = ROPE_THETA ** (-jnp.arange(half, dtype=F32) / half)
    ang = posf * inv[None, :]
    cos, sin = jnp.cos(ang), jnp.sin(ang)
    grp = lane % DIFF_HD
    sel = jnp.asarray(grp % half)
    cos_l, sin_l = cos[:, sel], sin[:, sel]
    dcos = jnp.where(jnp.asarray(grp < DIFF_ROT)[None, :], cos_l, 1.0)
    dsa = jnp.where(jnp.asarray((grp >= half) & (grp < DIFF_ROT))[None, :], sin_l, 0.0)
    dsb = jnp.where(jnp.asarray(grp < half)[None, :], -sin_l, 0.0)
    return tuple(jnp.tile(t.astype(F32), (reps, 1)) for t in (rcos, rsin, dcos, dsa, dsb))


def _ret_tables(c_len):
    log_g = jnp.log1p(-jnp.exp2(-5.0 - jnp.arange(N_RET_HEADS, dtype=F32)))
    idx = jnp.arange(c_len, dtype=F32)
    rel = idx[:, None] - idx[None, :]
    decay = jnp.where(rel >= 0, jnp.exp(log_g[:, None, None] * jnp.maximum(rel, 0.0)), 0.0)
    xi = jnp.exp(log_g[:, None] * (idx[None, :] + 1.0))
    zeta = jnp.exp(log_g[:, None] * (c_len - 1.0 - idx[None, :]))
    glen = jnp.exp(log_g * c_len)
    return (decay.astype(F32),
            jnp.broadcast_to(xi[:, :, None], (N_RET_HEADS, c_len, RET_DV)).astype(F32),
            jnp.broadcast_to(zeta[:, :, None], (N_RET_HEADS, c_len, RET_DK)).astype(F32),
            jnp.broadcast_to(glen[:, None, None], (N_RET_HEADS, 1, RET_DV)).astype(F32))


def _pick_tile(n, target):
    t = min(n, target)
    while n % t:
        t //= 2
    return t


def kernel(x_prompt, x_sample, cache_k, cache_v, state_ret, state_conv, norm1_g, w_in, ret_norm_g, q_norm_g,
           k_norm_g, lambda_q1, lambda_k1, lambda_q2, lambda_k2, diff_norm_g, w_branch_a, w_branch_b, w_out,
           norm2_g, w_up, conv_w, conv_b, w_down):
    bp, tp, d = x_prompt.shape
    bs, ls, _ = x_sample.shape
    depth = w_in.shape[0]
    past = cache_k.shape[2]
    d_ff = w_down.shape[1]
    hd2 = 2 * DIFF_HD

    tm_p = _pick_tile(tp, 1024)
    tn = 512
    c_p = _pick_tile(tp, 256)
    tq_p = _pick_tile(tp, 256)
    tm_glu = _pick_tile(tp, 512)
    ns = bs * ls
    tq_s = LANES
    tk_s = LANES
    lk_s = -(-(past + ls) // tk_s) * tk_s
    n_full_s = (past + ls - 1) // tk_s if (past + ls) % tk_s else (past + ls) // tk_s - 1
    n_valid_last_s = past + ls - n_full_s * tk_s

    pos_p = jnp.arange(tp, dtype=jnp.int32)
    pos_s = past + jnp.arange(ls, dtype=jnp.int32)
    tabs_p = _rope_tables(pos_p, 1)
    tabs_s = _rope_tables(pos_s, bs)
    rt_p = _ret_tables(c_p)
    rt_s = _ret_tables(ls)
    blk = np.arange(MXU_N) // DIFF_HD
    gmat = jnp.asarray(blk[:, None] == blk[None, :], dtype=BF16)

    yp = x_prompt.reshape(bp * tp, d)
    ys = x_sample.reshape(ns, d)
    outs = {k: [] for k in ("kp", "vp", "sp", "cp", "ks", "vs", "ss", "cs")}
    for l in range(depth):
        lam_init = 0.8 - 0.6 * math.exp(-0.3 * l)
        g1 = norm1_g[l].reshape(1, d)
        g2 = norm2_g[l].reshape(1, d)
        w16 = w_in[l].astype(BF16)
        wa16 = w_branch_a[l].astype(BF16)
        wb16 = w_branch_b[l].astype(BF16)
        wo16 = w_out[l].astype(BF16)
        wu16 = w_up[l].astype(BF16)
        wd16 = w_down[l].astype(BF16)
        qg = jnp.tile(q_norm_g[l], MXU_N // DIFF_HD).reshape(1, MXU_N)
        kg = jnp.tile(k_norm_g[l], MXU_N // DIFF_HD).reshape(1, MXU_N)
        rgain = ret_norm_g[l].reshape(1, RET_DV)
        dgain = diff_norm_g[l].reshape(1, hd2)
        lam_p = jnp.stack([lambda_q1[l], lambda_k1[l], lambda_q2[l], lambda_k2[l]])
        cw_l = conv_w[l]
        cb_l = conv_b[l].reshape(1, 2 * d_ff)

        p16, k32, v32 = _inproj(yp, g1, w16, tabs_p, qg, kg, gmat, tm=tm_p, tn=tn)
        ret16, s_fin = _retention(p16, rt_p, rgain, None, batch=bp, t_len=tp, c_len=c_p)
        dif16 = _attention(p16, p16, p16, lam_p, dgain, batch=bp, tq=tq_p, lq=tp, tk=tq_p, lk=tp,
                           qcol=OFF_QD // hd2, kcol=OFF_KD // hd2, vcol=OFF_VD // hd2, causal=True,
                           n_full=0, n_valid_last=0, lam_init=lam_init)
        x1, h2 = _merge(ret16, dif16, p16, yp, wa16, wb16, wo16, g2, tm=tm_p)
        yp, cst = _glu(h2, x1, wu16, wd16, cw_l, cb_l, None, batch=bp, t_len=tp, tm=tm_glu)
        outs["kp"].append(k32.reshape(bp, tp, N_DIFF_HEADS, hd2))
        outs["vp"].append(v32.reshape(bp, tp, N_DIFF_HEADS, hd2))
        outs["sp"].append(s_fin)
        outs["cp"].append(cst)

        p16, k32, v32 = _inproj(ys, g1, w16, tabs_s, qg, kg, gmat, tm=ns, tn=tn)
        ret16, s_fin = _retention(p16, rt_s, rgain, state_ret[l], batch=bs, t_len=ls, c_len=ls)
        q_pad = jnp.pad(p16[:, OFF_QD:OFF_QD + DIFF_W].reshape(bs, ls, DIFF_W), ((0, 0), (0, tq_s - ls), (0, 0)))
        pad_k = lk_s - past - ls
        k_all = jnp.concatenate([cache_k[l].reshape(bs, past, DIFF_W).astype(BF16),
                                 p16[:, OFF_KD:OFF_KD + DIFF_W].reshape(bs, ls, DIFF_W),
                                 jnp.zeros((bs, pad_k, DIFF_W), BF16)], axis=1)
        v_all = jnp.concatenate([cache_v[l].reshape(bs, past, DIFF_W).astype(BF16),
                                 p16[:, OFF_VD:OFF_VD + DIFF_W].reshape(bs, ls, DIFF_W),
                                 jnp.zeros((bs, pad_k, DIFF_W), BF16)], axis=1)
        dif_pad = _attention(q_pad.reshape(bs * tq_s, DIFF_W), k_all.reshape(bs * lk_s, DIFF_W),
                             v_all.reshape(bs * lk_s, DIFF_W), lam_p, dgain, batch=bs, tq=tq_s, lq=tq_s,
                             tk=tk_s, lk=lk_s, qcol=0, kcol=0, vcol=0, causal=False, n_full=n_full_s,
                             n_valid_last=n_valid_last_s, lam_init=lam_init)
        dif16 = dif_pad.reshape(bs, tq_s, DIFF_W)[:, :ls].reshape(ns, DIFF_W)
        x1, h2 = _merge(ret16, dif16, p16, ys, wa16, wb16, wo16, g2, tm=ns)
        ys, cst = _glu(h2, x1, wu16, wd16, cw_l, cb_l, state_conv[l], batch=bs, t_len=ls, tm=ls)
        outs["ks"].append(k32.reshape(bs, ls, N_DIFF_HEADS, hd2))
        outs["vs"].append(v32.reshape(bs, ls, N_DIFF_HEADS, hd2))
        outs["ss"].append(s_fin)
        outs["cs"].append(cst)

    st = {k: jnp.stack(v) for k, v in outs.items()}
    return (yp.reshape(bp, tp, d), ys.reshape(bs, ls, d), st["kp"], st["vp"], st["sp"], st["cp"],
            st["ks"], st["vs"], st["ss"], st["cs"])
```

```python
import functools
import math

import jax
import jax.numpy as jnp
import numpy as np
from jax import lax
from jax.experimental import pallas as pl
from jax.experimental.pallas import tpu as pltpu

F32 = jnp.float32
BF16 = jnp.bfloat16

EPS = 1e-6
CHUNK = 64
N_RET_HEADS = 4
RET_DK = 128
RET_DV = 256
RET_ROPE_THETA = 10000.0
N_DIFF_HEADS = 8
DIFF_HD = 64
DIFF_ROT = DIFF_HD // 4
ROPE_THETA = 500000.0
CONV_W = 3

RET_QK = N_RET_HEADS * RET_DK
RET_V = N_RET_HEADS * RET_DV
DIFF_W = N_DIFF_HEADS * 2 * DIFF_HD
OFF_QR = 0
OFF_KR = OFF_QR + RET_QK
OFF_VR = OFF_KR + RET_QK
OFF_GR = OFF_VR + RET_V
OFF_QD = OFF_GR + RET_V
OFF_KD = OFF_QD + DIFF_W
OFF_VD = OFF_KD + DIFF_W
OFF_GA = OFF_VD + DIFF_W

LANES = 128
MXU_N = 256
VMEM_LIMIT = 56 * 1024 * 1024
NEG_BIG = -1e30
LOG2E = math.log2(math.e)


def _sigmoid(x):
    return 1.0 / (1.0 + jnp.exp(-x))


def _dot(a, b):
    return jnp.dot(a, b, preferred_element_type=F32)


def _dot_nt(a, b):
    return lax.dot_general(a, b, (((1,), (1,)), ((), ())), preferred_element_type=F32)


def _dot_tn(a, b):
    return lax.dot_general(a, b, (((0,), (0,)), ((), ())), preferred_element_type=F32)


def _params(*sem):
    return pltpu.CompilerParams(dimension_semantics=sem, vmem_limit_bytes=VMEM_LIMIT)


def _rms(x, g):
    ms = jnp.mean(x * x, axis=-1, keepdims=True)
    return x * lax.rsqrt(ms + EPS) * g


def _norm_kernel(x_ref, g_ref, h_ref):
    h_ref[...] = _rms(x_ref[...], g_ref[...]).astype(BF16)


def _norm(x, g, *, tm):
    n, d = x.shape
    return pl.pallas_call(
        _norm_kernel,
        grid=(n // tm,),
        in_specs=[pl.BlockSpec((tm, d), lambda i: (i, 0)), pl.BlockSpec((1, d), lambda i: (0, 0))],
        out_specs=pl.BlockSpec((tm, d), lambda i: (i, 0)),
        out_shape=jax.ShapeDtypeStruct((n, d), BF16),
        compiler_params=_params("arbitrary"),
        name="norm",
    )(x, g)


def _proj_ret_kernel(h_ref, w_ref, rcos_ref, rsin_ref, o_ref):
    y = _dot(h_ref[...], w_ref[...])
    for c in range(y.shape[1] // LANES):
        ys = y[:, c * LANES:(c + 1) * LANES]
        r = ys * rcos_ref[...] + pltpu.roll(ys, RET_DK // 2, 1) * rsin_ref[...]
        if c * LANES >= RET_QK:
            r = r * (RET_DK ** -0.5)
        o_ref[:, c * LANES:(c + 1) * LANES] = r.astype(BF16)


def _proj_act_kernel(h_ref, w_ref, o_ref):
    j = pl.program_id(1)
    y = _dot(h_ref[...], w_ref[...])
    sg = _sigmoid(y)
    o_ref[...] = jnp.where(j == 0, y, jnp.where(j == 1, y * sg, sg)).astype(BF16)


def _proj_diff_kernel(h_ref, w_ref, dcos_ref, dsa_ref, dsb_ref, gain_ref, gmat_ref, o16_ref, o32_ref):
    j = pl.program_id(1)
    y = _dot(h_ref[...], w_ref[...])
    scale = jnp.where(j == 0, DIFF_HD ** -0.5 * LOG2E, 1.0).astype(F32)
    for c in range(y.shape[1] // MXU_N):
        ys = y[:, c * MXU_N:(c + 1) * MXU_N]
        ss = _dot((ys * ys).astype(BF16), gmat_ref[...])
        yn = ys * lax.rsqrt(ss * (1.0 / DIFF_HD) + EPS) * gain_ref[0]
        for d in range(MXU_N // LANES):
            z = yn[:, d * LANES:(d + 1) * LANES]
            r = (z * dcos_ref[...] + pltpu.roll(z, DIFF_ROT // 2, 1) * dsa_ref[...]
                 + pltpu.roll(z, LANES - DIFF_ROT // 2, 1) * dsb_ref[...])
            lo = c * MXU_N + d * LANES
            o32_ref[:, lo:lo + LANES] = r
            o16_ref[:, lo:lo + LANES] = (r * scale).astype(BF16)


def _proj_val_kernel(h_ref, w_ref, o16_ref, o32_ref):
    y = _dot(h_ref[...], w_ref[...])
    o32_ref[...] = y
    o16_ref[...] = y.astype(BF16)


def _proj(h16, w16, tabs, qkg, gmat, *, tm, tn):
    n, d = h16.shape
    rcos, rsin, dcos, dsa, dsb = tabs
    nt = rcos.shape[0] // tm
    h_spec = pl.BlockSpec((tm, d), lambda i, j: (i, 0))
    tab_spec = pl.BlockSpec((tm, LANES), lambda i, j: (i % nt, 0))
    o_spec = pl.BlockSpec((tm, tn), lambda i, j: (i, j))
    sem = ("arbitrary", "arbitrary")
    assert tn == 2 * RET_QK == RET_V == DIFF_W == d

    def w_spec(col0):
        return pl.BlockSpec((d, tn), lambda i, j: (0, col0 // tn + j))

    qk_r = pl.pallas_call(
        _proj_ret_kernel, grid=(n // tm, 1),
        in_specs=[h_spec, w_spec(OFF_QR), tab_spec, tab_spec],
        out_specs=o_spec, out_shape=jax.ShapeDtypeStruct((n, tn), BF16),
        compiler_params=_params(*sem), name="proj_ret",
    )(h16, w16, rcos, rsin)

    act_w = pl.BlockSpec((d, tn), lambda i, j: (0, jnp.where(j < 2, OFF_VR // tn + j, OFF_GA // tn + j - 2)))
    act = pl.pallas_call(
        _proj_act_kernel, grid=(n // tm, 4),
        in_specs=[h_spec, act_w],
        out_specs=o_spec, out_shape=jax.ShapeDtypeStruct((n, 4 * tn), BF16),
        compiler_params=_params(*sem), name="proj_act",
    )(h16, w16)

    qk_d, k32 = pl.pallas_call(
        _proj_diff_kernel, grid=(n // tm, 2),
        in_specs=[h_spec, w_spec(OFF_QD), tab_spec, tab_spec, tab_spec,
                  pl.BlockSpec((1, 1, MXU_N), lambda i, j: (j, 0, 0)),
                  pl.BlockSpec((MXU_N, MXU_N), lambda i, j: (0, 0))],
        out_specs=[o_spec, pl.BlockSpec((tm, tn), lambda i, j: (i, 0))],
        out_shape=[jax.ShapeDtypeStruct((n, 2 * tn), BF16), jax.ShapeDtypeStruct((n, tn), F32)],
        compiler_params=_params(*sem), name="proj_diff",
    )(h16, w16, dcos, dsa, dsb, qkg, gmat)

    v16, v32 = pl.pallas_call(
        _proj_val_kernel, grid=(n // tm, 1),
        in_specs=[h_spec, w_spec(OFF_VD)],
        out_specs=[o_spec, o_spec],
        out_shape=[jax.ShapeDtypeStruct((n, tn), BF16), jax.ShapeDtypeStruct((n, tn), F32)],
        compiler_params=_params(*sem), name="proj_val",
    )(h16, w16)
    return qk_r, act, qk_d, k32, v16, v32


def _ret_kernel(*refs, c_len, n_chunks, has_state):
    if has_state:
        (q_ref, k_ref, v_ref, g_ref, decay_ref, xi_ref, zeta_ref, glen_ref, gain_ref, s0_ref,
         o_ref, s_ref) = refs
        s = s0_ref[0, 0]
    else:
        (q_ref, k_ref, v_ref, g_ref, decay_ref, xi_ref, zeta_ref, glen_ref, gain_ref,
         o_ref, s_ref) = refs
        s = jnp.zeros((RET_DK, RET_DV), F32)
    decay = decay_ref[0]
    for c in range(n_chunks):
        rows = slice(c * c_len, (c + 1) * c_len)
        q = q_ref[rows, :]
        k = k_ref[rows, :]
        v = v_ref[rows, :]
        sc = _dot_nt(q, k) * decay
        o = _dot(sc.astype(BF16), v) + _dot(q, s.astype(BF16)) * xi_ref[0]
        kz = (k.astype(F32) * zeta_ref[0]).astype(BF16)
        s = s * glen_ref[0] + _dot_tn(kz, v)
        ms = jnp.mean(o * o, axis=-1, keepdims=True)
        on = o * lax.rsqrt(ms + EPS) * gain_ref[...]
        o_ref[rows, :] = (on * g_ref[rows, :].astype(F32)).astype(BF16)
    s_ref[0, 0] = s


def _retention(qk_r, act, tables, gain, s0, *, batch, t_len, c_len):
    n = qk_r.shape[0]
    decay, xi, zeta, glen = tables
    has_state = s0 is not None
    in_specs = [
        pl.BlockSpec((t_len, RET_DK), lambda b, h: (b, h)),
        pl.BlockSpec((t_len, RET_DK), lambda b, h: (b, N_RET_HEADS + h)),
        pl.BlockSpec((t_len, RET_DV), lambda b, h: (b, h)),
        pl.BlockSpec((t_len, RET_DV), lambda b, h: (b, N_RET_HEADS + h)),
        pl.BlockSpec((1, c_len, c_len), lambda b, h: (h, 0, 0)),
        pl.BlockSpec((1, c_len, RET_DV), lambda b, h: (h, 0, 0)),
        pl.BlockSpec((1, c_len, RET_DK), lambda b, h: (h, 0, 0)),
        pl.BlockSpec((1, 1, RET_DV), lambda b, h: (h, 0, 0)),
        pl.BlockSpec((1, RET_DV), lambda b, h: (0, 0)),
    ]
    args = [qk_r, qk_r, act, act, decay, xi, zeta, glen, gain]
    if has_state:
        in_specs.append(pl.BlockSpec((1, 1, RET_DK, RET_DV), lambda b, h: (b, h, 0, 0)))
        args.append(s0)
    return pl.pallas_call(
        functools.partial(_ret_kernel, c_len=c_len, n_chunks=t_len // c_len, has_state=has_state),
        grid=(batch, N_RET_HEADS),
        in_specs=in_specs,
        out_specs=[
            pl.BlockSpec((t_len, RET_DV), lambda b, h: (b, h)),
            pl.BlockSpec((1, 1, RET_DK, RET_DV), lambda b, h: (b, h, 0, 0)),
        ],
        out_shape=[
            jax.ShapeDtypeStruct((n, RET_V), BF16),
            jax.ShapeDtypeStruct((batch, N_RET_HEADS, RET_DK, RET_DV), F32),
        ],
        compiler_params=_params("arbitrary", "arbitrary"),
        name="ret",
    )(*args)


def _attn_kernel(q_ref, k_ref, v_ref, lam_ref, gain_ref, o_ref, vt_scr, *, blocks, tb, causal, n_valid_last,
                 lam_init):
    lk = k_ref.shape[0]
    for t in range(lk // tb):
        vt_scr[:, t * tb:(t + 1) * tb] = v_ref[t * tb:(t + 1) * tb, :].astype(F32).T.astype(BF16)
    lp = lam_ref[...]
    lam = (jnp.exp(jnp.sum(lp[0:1] * lp[1:2], axis=1, keepdims=True))
           - jnp.exp(jnp.sum(lp[2:3] * lp[3:4], axis=1, keepdims=True)) + lam_init)

    for q0, tq, nk in blocks:
        kidx = lax.broadcasted_iota(jnp.int32, (tb, tq), 0)
        if causal:
            qidx = lax.broadcasted_iota(jnp.int32, (tb, tq), 1)
            visible = (kidx // CHUNK) <= (qidx // CHUNK)
        else:
            visible = kidx < n_valid_last
        bias = jnp.where(visible, 0.0, NEG_BIG).astype(F32)
        q = q_ref[q0:q0 + tq, :]
        lane = lax.broadcasted_iota(jnp.int32, q.shape, 1)
        nm = nk - tb
        res = []
        for c in range(2):
            in_map = (lane < DIFF_HD) if c == 0 else (lane >= DIFF_HD)
            qc = jnp.where(in_map, q, jnp.zeros_like(q))
            sd = _dot_nt(k_ref[nm:nk, :], qc) + bias
            m = jnp.max(sd, axis=0, keepdims=True)
            if nm > 0:
                sm = _dot_nt(k_ref[0:nm, :], qc)
                m = jnp.maximum(m, jnp.max(sm, axis=0, keepdims=True))
            pd = jnp.exp2(sd - m)
            l = jnp.sum(pd, axis=0, keepdims=True)
            acc = _dot(vt_scr[:, nm:nk], pd.astype(BF16))
            if nm > 0:
                pm = jnp.exp2(sm - m)
                l = l + jnp.sum(pm, axis=0, keepdims=True)
                acc = acc + _dot(vt_scr[:, 0:nm], pm.astype(BF16))
            res.append((l, acc))
        (l0, a0), (l1, a1) = res
        o = a0 * (1.0 / l0) - (lam * (1.0 / l1)) * a1
        ms = jnp.mean(o * o, axis=0, keepdims=True)
        on = (o * lax.rsqrt(ms + EPS)).T
        o_ref[q0:q0 + tq, :] = (on * gain_ref[...] * (1.0 - lam_init)).astype(BF16)


def _attention(q_arr, k_arr, v_arr, lam_p, gain, *, batch, lq, lk, qcol, kcol, vcol, blocks, tb, causal,
               n_valid_last, lam_init):
    hd2 = 2 * DIFF_HD
    return pl.pallas_call(
        functools.partial(_attn_kernel, blocks=blocks, tb=tb, causal=causal, n_valid_last=n_valid_last,
                          lam_init=lam_init),
        grid=(batch, N_DIFF_HEADS),
        in_specs=[
            pl.BlockSpec((lq, hd2), lambda b, h: (b, qcol + h)),
            pl.BlockSpec((lk, hd2), lambda b, h: (b, kcol + h)),
            pl.BlockSpec((lk, hd2), lambda b, h: (b, vcol + h)),
            pl.BlockSpec((4, DIFF_HD), lambda b, h: (0, 0)),
            pl.BlockSpec((1, hd2), lambda b, h: (0, 0)),
        ],
        out_specs=pl.BlockSpec((lq, hd2), lambda b, h: (b, h)),
        out_shape=jax.ShapeDtypeStruct((batch * lq, DIFF_W), BF16),
        scratch_shapes=[pltpu.VMEM((hd2, lk), BF16)],
        compiler_params=_params("arbitrary", "arbitrary"),
        name="attn",
    )(q_arr, k_arr, v_arr, lam_p, gain)


def _merge_kernel(ret_ref, dif_ref, sa_ref, sb_ref, x_ref, wa_ref, wb_ref, wo_ref, g2_ref, x1_ref, h2_ref):
    ya = _dot(ret_ref[...], wa_ref[...])
    yb = _dot(dif_ref[...], wb_ref[...])
    y = sa_ref[...].astype(F32) * ya + sb_ref[...].astype(F32) * yb
    x1 = x_ref[...] + _dot(y.astype(BF16), wo_ref[...])
    x1_ref[...] = x1
    h2_ref[...] = _rms(x1, g2_ref[...]).astype(BF16)


def _merge(ret16, dif16, act, x, wa16, wb16, wo16, g2, *, tm):
    n, d = x.shape
    full = lambda i: (0, 0)
    return pl.pallas_call(
        _merge_kernel,
        grid=(n // tm,),
        in_specs=[
            pl.BlockSpec((tm, RET_V), lambda i: (i, 0)),
            pl.BlockSpec((tm, DIFF_W), lambda i: (i, 0)),
            pl.BlockSpec((tm, d), lambda i: (i, 2)),
            pl.BlockSpec((tm, d), lambda i: (i, 3)),
            pl.BlockSpec((tm, d), lambda i: (i, 0)),
            pl.BlockSpec((RET_V, d), full),
            pl.BlockSpec((DIFF_W, d), full),
            pl.BlockSpec((d, d), full),
            pl.BlockSpec((1, d), full),
        ],
        out_specs=[pl.BlockSpec((tm, d), lambda i: (i, 0)), pl.BlockSpec((tm, d), lambda i: (i, 0))],
        out_shape=[jax.ShapeDtypeStruct((n, d), F32), jax.ShapeDtypeStruct((n, d), BF16)],
        compiler_params=_params("arbitrary"),
        name="merge",
    )(ret16, dif16, act, act, x, wa16, wb16, wo16, g2)


def _glu_kernel(*refs, tm, n_t, d_ff, cw, has_state, has_next):
    refs = list(refs)
    h_ref, x1_ref, wu_ref, wd_ref, cwt_ref, cb_ref = refs[:6]
    del refs[:6]
    st_ref = refs.pop(0) if has_state else None
    gn_ref = refs.pop(0) if has_next else None
    y_ref, cs_ref = refs[:2]
    del refs[:2]
    hn_ref = refs.pop(0) if has_next else None
    carry, ua_scr, ug_scr = refs
    ti = pl.program_id(0) % n_t

    @pl.when(ti == 0)
    def _():
        carry[...] = jnp.zeros_like(carry)
        if has_state:
            carry[8 - (CONV_W - 1):8, :] = st_ref[0]

    h = h_ref[...]
    acc = jnp.zeros(y_ref.shape, F32)
    for c in range(d_ff // cw):
        conv = []
        for scr, base in ((ua_scr, c * cw), (ug_scr, d_ff + c * cw)):
            cols = slice(base, base + cw)
            u = _dot(h, wu_ref[:, cols])
            scr[0:8, :] = carry[:, cols]
            scr[8:8 + tm, :] = u
            carry[8 - (CONV_W - 1):8, cols] = u[tm - (CONV_W - 1):tm, :]
            r = cb_ref[:, cols] + cwt_ref[0:1, cols] * scr[6:6 + tm, :]
            r = r + cwt_ref[1:2, cols] * scr[7:7 + tm, :]
            r = r + cwt_ref[2:3, cols] * u
            conv.append(r)
        a, g = conv
        act = (g * _sigmoid(g) * a).astype(BF16)
        acc = acc + _dot(act, wd_ref[c * cw:(c + 1) * cw, :])
    y = x1_ref[...] + acc
    y_ref[...] = y
    if has_next:
        hn_ref[...] = _rms(y, gn_ref[...]).astype(BF16)

    @pl.when(ti == n_t - 1)
    def _():
        cs_ref[0] = carry[8 - (CONV_W - 1):8, :]


def _glu(h2, x1, wu16, wd16, conv_w, conv_b, state, g_next, *, batch, t_len, tm):
    n, d = x1.shape
    d_ff = wd16.shape[0]
    n_t = t_len // tm
    has_state = state is not None
    has_next = g_next is not None
    full = lambda i: (0, 0)
    row = pl.BlockSpec((tm, d), lambda i: (i, 0))
    in_specs = [row, row, pl.BlockSpec((d, 2 * d_ff), full), pl.BlockSpec((d_ff, d), full),
                pl.BlockSpec((CONV_W, 2 * d_ff), full), pl.BlockSpec((1, 2 * d_ff), full)]
    args = [h2, x1, wu16, wd16, conv_w, conv_b]
    st_spec = pl.BlockSpec((1, CONV_W - 1, 2 * d_ff), lambda i: (i // n_t, 0, 0))
    if has_state:
        in_specs.append(st_spec)
        args.append(state)
    out_specs = [row, st_spec]
    out_shape = [jax.ShapeDtypeStruct((n, d), F32), jax.ShapeDtypeStruct((batch, CONV_W - 1, 2 * d_ff), F32)]
    if has_next:
        in_specs.append(pl.BlockSpec((1, d), full))
        args.append(g_next)
        out_specs.append(row)
        out_shape.append(jax.ShapeDtypeStruct((n, d), BF16))
    res = pl.pallas_call(
        functools.partial(_glu_kernel, tm=tm, n_t=n_t, d_ff=d_ff, cw=MXU_N, has_state=has_state,
                          has_next=has_next),
        grid=(n // tm,),
        in_specs=in_specs,
        out_specs=out_specs,
        out_shape=out_shape,
        scratch_shapes=[
            pltpu.VMEM((8, 2 * d_ff), F32),
            pltpu.VMEM((8 + tm, MXU_N), F32),
            pltpu.VMEM((8 + tm, MXU_N), F32),
        ],
        compiler_params=_params("arbitrary"),
        name="glu",
    )(*args)
    return res if has_next else (*res, None)


def _rope_tables(pos, reps):
    posf = pos.astype(F32)[:, None]
    lane = np.arange(LANES)
    half = RET_DK // 2
    inv = RET_ROPE_THETA ** (-jnp.arange(half, dtype=F32) / half)
    ang = posf * inv[None, :]
    cos, sin = jnp.cos(ang), jnp.sin(ang)
    rcos = jnp.concatenate([cos, cos], axis=1)
    rsin = jnp.concatenate([-sin, sin], axis=1)
    half = DIFF_ROT // 2
    inv = ROPE_THETA ** (-jnp.arange(half, dtype=F32) / half)
    ang = posf * inv[None, :]
    cos, sin = jnp.cos(ang), jnp.sin(ang)
    grp = lane % DIFF_HD
    sel = jnp.asarray(grp % half)
    cos_l, sin_l = cos[:, sel], sin[:, sel]
    dcos = jnp.where(jnp.asarray(grp < DIFF_ROT)[None, :], cos_l, 1.0)
    dsa = jnp.where(jnp.asarray((grp >= half) & (grp < DIFF_ROT))[None, :], sin_l, 0.0)
    dsb = jnp.where(jnp.asarray(grp < half)[None, :], -sin_l, 0.0)
    return tuple(jnp.tile(t.astype(F32), (reps, 1)) for t in (rcos, rsin, dcos, dsa, dsb))


def _ret_tables(c_len):
    log_g = jnp.log1p(-jnp.exp2(-5.0 - jnp.arange(N_RET_HEADS, dtype=F32)))
    idx = jnp.arange(c_len, dtype=F32)
    rel = idx[:, None] - idx[None, :]
    decay = jnp.where(rel >= 0, jnp.exp(log_g[:, None, None] * jnp.maximum(rel, 0.0)), 0.0)
    xi = jnp.exp(log_g[:, None] * (idx[None, :] + 1.0))
    zeta = jnp.exp(log_g[:, None] * (c_len - 1.0 - idx[None, :]))
    glen = jnp.exp(log_g * c_len)
    return (decay.astype(F32),
            jnp.broadcast_to(xi[:, :, None], (N_RET_HEADS, c_len, RET_DV)).astype(F32),
            jnp.broadcast_to(zeta[:, :, None], (N_RET_HEADS, c_len, RET_DK)).astype(F32),
            jnp.broadcast_to(glen[:, None, None], (N_RET_HEADS, 1, RET_DV)).astype(F32))


def _pick_tile(n, target):
    t = min(n, target)
    while n % t:
        t //= 2
    return t


def kernel(x_prompt, x_sample, cache_k, cache_v, state_ret, state_conv, norm1_g, w_in, ret_norm_g, q_norm_g,
           k_norm_g, lambda_q1, lambda_k1, lambda_q2, lambda_k2, diff_norm_g, w_branch_a, w_branch_b, w_out,
           norm2_g, w_up, conv_w, conv_b, w_down):
    bp, tp, d = x_prompt.shape
    bs, ls, _ = x_sample.shape
    depth = w_in.shape[0]
    past = cache_k.shape[2]
    d_ff = w_down.shape[1]
    hd2 = 2 * DIFF_HD

    tm_p = _pick_tile(tp, 1024)
    tn = d
    c_p = _pick_tile(tp, 256)
    tq_p = _pick_tile(tp, 256)
    tm_glu = _pick_tile(tp, 512)
    ns = bs * ls
    tq_s = LANES
    lk_s = -(-(past + ls) // LANES) * LANES
    n_valid_last_s = past + ls - (lk_s - LANES)
    blocks_p = tuple((i * tq_p, tq_p, (i + 1) * tq_p) for i in range(tp // tq_p))
    blocks_s = ((0, tq_s, lk_s),)

    pos_p = jnp.arange(tp, dtype=jnp.int32)
    pos_s = past + jnp.arange(ls, dtype=jnp.int32)
    tabs_p = _rope_tables(pos_p, 1)
    tabs_s = _rope_tables(pos_s, bs)
    rt_p = _ret_tables(c_p)
    rt_s = _ret_tables(ls)
    blk = np.arange(MXU_N) // DIFF_HD
    gmat = jnp.asarray(blk[:, None] == blk[None, :], dtype=BF16)

    yp = x_prompt.reshape(bp * tp, d)
    ys = x_sample.reshape(ns, d)
    hp = _norm(yp, norm1_g[0].reshape(1, d), tm=tm_p)
    hs = _norm(ys, norm1_g[0].reshape(1, d), tm=ns)
    outs = {k: [] for k in ("kp", "vp", "sp", "cp", "ks", "vs", "ss", "cs")}
    for l in range(depth):
        lam_init = 0.8 - 0.6 * math.exp(-0.3 * l)
        g2 = norm2_g[l].reshape(1, d)
        g_next = norm1_g[l + 1].reshape(1, d) if l + 1 < depth else None
        w16 = w_in[l].astype(BF16)
        wa16 = w_branch_a[l].astype(BF16)
        wb16 = w_branch_b[l].astype(BF16)
        wo16 = w_out[l].astype(BF16)
        wu16 = w_up[l].astype(BF16)
        wd16 = w_down[l].astype(BF16)
        reps = MXU_N // DIFF_HD
        qkg = jnp.stack([jnp.tile(q_norm_g[l], reps), jnp.tile(k_norm_g[l], reps)]).reshape(2, 1, MXU_N)
        rgain = ret_norm_g[l].reshape(1, RET_DV)
        dgain = diff_norm_g[l].reshape(1, hd2)
        lam_p = jnp.stack([lambda_q1[l], lambda_k1[l], lambda_q2[l], lambda_k2[l]])
        cw_l = conv_w[l]
        cb_l = conv_b[l].reshape(1, 2 * d_ff)

        qk_r, act, qk_d, k32, v16, v32 = _proj(hp, w16, tabs_p, qkg, gmat, tm=tm_p, tn=tn)
        ret16, s_fin = _retention(qk_r, act, rt_p, rgain, None, batch=bp, t_len=tp, c_len=c_p)
        dif16 = _attention(qk_d, qk_d, v16, lam_p, dgain, batch=bp, lq=tp, lk=tp, qcol=0,
                           kcol=N_DIFF_HEADS, vcol=0, blocks=blocks_p, tb=tq_p, causal=True,
                           n_valid_last=0, lam_init=lam_init)
        x1, h2 = _merge(ret16, dif16, act, yp, wa16, wb16, wo16, g2, tm=tm_p)
        yp, cst, hp = _glu(h2, x1, wu16, wd16, cw_l, cb_l, None, g_next, batch=bp, t_len=tp, tm=tm_glu)
        outs["kp"].append(k32.reshape(bp, tp, N_DIFF_HEADS, hd2))
        outs["vp"].append(v32.reshape(bp, tp, N_DIFF_HEADS, hd2))
        outs["sp"].append(s_fin)
        outs["cp"].append(cst)

        qk_r, act, qk_d, k32, v16, v32 = _proj(hs, w16, tabs_s, qkg, gmat, tm=ns, tn=tn)
        ret16, s_fin = _retention(qk_r, act, rt_s, rgain, state_ret[l], batch=bs, t_len=ls, c_len=ls)
        q_pad = jnp.pad(qk_d[:, :DIFF_W].reshape(bs, ls, DIFF_W), ((0, 0), (0, tq_s - ls), (0, 0)))
        pad_k = lk_s - past - ls
        k_all = jnp.concatenate([cache_k[l].reshape(bs, past, DIFF_W).astype(BF16),
                                 qk_d[:, DIFF_W:].reshape(bs, ls, DIFF_W),
                                 jnp.zeros((bs, pad_k, DIFF_W), BF16)], axis=1)
        v_all = jnp.concatenate([cache_v[l].reshape(bs, past, DIFF_W).astype(BF16),
                                 v16.reshape(bs, ls, DIFF_W),
                                 jnp.zeros((bs, pad_k, DIFF_W), BF16)], axis=1)
        dif_pad = _attention(q_pad.reshape(bs * tq_s, DIFF_W), k_all.reshape(bs * lk_s, DIFF_W),
                             v_all.reshape(bs * lk_s, DIFF_W), lam_p, dgain, batch=bs, lq=tq_s, lk=lk_s,
                             qcol=0, kcol=0, vcol=0, blocks=blocks_s, tb=LANES, causal=False,
                             n_valid_last=n_valid_last_s, lam_init=lam_init)
        dif16 = dif_pad.reshape(bs, tq_s, DIFF_W)[:, :ls].reshape(ns, DIFF_W)
        x1, h2 = _merge(ret16, dif16, act, ys, wa16, wb16, wo16, g2, tm=ns)
        ys, cst, hs = _glu(h2, x1, wu16, wd16, cw_l, cb_l, state_conv[l], g_next, batch=bs, t_len=ls, tm=ls)
        outs["ks"].append(k32.reshape(bs, ls, N_DIFF_HEADS, hd2))
        outs["vs"].append(v32.reshape(bs, ls, N_DIFF_HEADS, hd2))
        outs["ss"].append(s_fin)
        outs["cs"].append(cst)

    st = {k: jnp.stack(v) for k, v in outs.items()}
    return (yp.reshape(bp, tp, d), ys.reshape(bs, ls, d), st["kp"], st["vp"], st["sp"], st["cp"],
            st["ks"], st["vs"], st["ss"], st["cs"])
```

```python
import functools
import math

import jax
import jax.numpy as jnp
import numpy as np
from jax import lax
from jax.experimental import pallas as pl
from jax.experimental.pallas import tpu as pltpu

F32 = jnp.float32
BF16 = jnp.bfloat16

EPS = 1e-6
CHUNK = 64
N_RET_HEADS = 4
RET_DK = 128
RET_DV = 256
RET_ROPE_THETA = 10000.0
N_DIFF_HEADS = 8
DIFF_HD = 64
DIFF_ROT = DIFF_HD // 4
ROPE_THETA = 500000.0
CONV_W = 3

RET_QK = N_RET_HEADS * RET_DK
RET_V = N_RET_HEADS * RET_DV
DIFF_W = N_DIFF_HEADS * 2 * DIFF_HD
OFF_QR = 0
OFF_KR = OFF_QR + RET_QK
OFF_VR = OFF_KR + RET_QK
OFF_GR = OFF_VR + RET_V
OFF_QD = OFF_GR + RET_V
OFF_KD = OFF_QD + DIFF_W
OFF_VD = OFF_KD + DIFF_W
OFF_GA = OFF_VD + DIFF_W

LANES = 128
MXU_N = 256
VMEM_LIMIT = 56 * 1024 * 1024
NEG_BIG = -1e30
LOG2E = math.log2(math.e)


def _sigmoid(x):
    return 1.0 / (1.0 + jnp.exp(-x))


def _dot(a, b):
    return jnp.dot(a, b, preferred_element_type=F32)


def _dot_nt(a, b):
    return lax.dot_general(a, b, (((1,), (1,)), ((), ())), preferred_element_type=F32)


def _dot_tn(a, b):
    return lax.dot_general(a, b, (((0,), (0,)), ((), ())), preferred_element_type=F32)


def _params(*sem):
    return pltpu.CompilerParams(dimension_semantics=sem, vmem_limit_bytes=VMEM_LIMIT)


def _rms(x, g):
    ms = jnp.mean(x * x, axis=-1, keepdims=True)
    return x * lax.rsqrt(ms + EPS) * g


def _norm_kernel(x_ref, g_ref, h_ref):
    h_ref[...] = _rms(x_ref[...], g_ref[...]).astype(BF16)


def _norm(x, g, *, tm):
    n, d = x.shape
    return pl.pallas_call(
        _norm_kernel,
        grid=(n // tm,),
        in_specs=[pl.BlockSpec((tm, d), lambda i: (i, 0)), pl.BlockSpec((1, d), lambda i: (0, 0))],
        out_specs=pl.BlockSpec((tm, d), lambda i: (i, 0)),
        out_shape=jax.ShapeDtypeStruct((n, d), BF16),
        compiler_params=_params("arbitrary"),
        name="norm",
    )(x, g)


def _proj_ret_kernel(h_ref, w_ref, rcos_ref, rsin_ref, o_ref):
    y = _dot(h_ref[...], w_ref[...])
    for c in range(y.shape[1] // LANES):
        ys = y[:, c * LANES:(c + 1) * LANES]
        r = ys * rcos_ref[...] + pltpu.roll(ys, RET_DK // 2, 1) * rsin_ref[...]
        if c * LANES >= RET_QK:
            r = r * (RET_DK ** -0.5)
        o_ref[:, c * LANES:(c + 1) * LANES] = r.astype(BF16)


def _proj_act_kernel(h_ref, w_ref, o_ref):
    j = pl.program_id(1)
    y = _dot(h_ref[...], w_ref[...])
    sg = _sigmoid(y)
    o_ref[...] = jnp.where(j == 0, y, jnp.where(j == 1, y * sg, sg)).astype(BF16)


def _proj_diff_kernel(h_ref, w_ref, dcos_ref, dsa_ref, dsb_ref, gain_ref, gmat_ref, o16_ref, o32_ref):
    j = pl.program_id(1)
    y = _dot(h_ref[...], w_ref[...])
    scale = jnp.where(j == 0, DIFF_HD ** -0.5 * LOG2E, 1.0).astype(F32)
    for c in range(y.shape[1] // MXU_N):
        ys = y[:, c * MXU_N:(c + 1) * MXU_N]
        ss = _dot((ys * ys).astype(BF16), gmat_ref[...])
        yn = ys * lax.rsqrt(ss * (1.0 / DIFF_HD) + EPS) * gain_ref[0]
        for d in range(MXU_N // LANES):
            z = yn[:, d * LANES:(d + 1) * LANES]
            r = (z * dcos_ref[...] + pltpu.roll(z, DIFF_ROT // 2, 1) * dsa_ref[...]
                 + pltpu.roll(z, LANES - DIFF_ROT // 2, 1) * dsb_ref[...])
            lo = c * MXU_N + d * LANES
            o32_ref[:, lo:lo + LANES] = r
            o16_ref[:, lo:lo + LANES] = (r * scale).astype(BF16)


def _proj_val_kernel(h_ref, w_ref, o16_ref, o32_ref):
    y = _dot(h_ref[...], w_ref[...])
    o32_ref[...] = y
    o16_ref[...] = y.astype(BF16)


def _proj(h16, w16, tabs, qkg, gmat, *, tm, tn):
    n, d = h16.shape
    rcos, rsin, dcos, dsa, dsb = tabs
    nt = rcos.shape[0] // tm
    h_spec = pl.BlockSpec((tm, d), lambda i, j: (i, 0))
    tab_spec = pl.BlockSpec((tm, LANES), lambda i, j: (i % nt, 0))
    o_spec = pl.BlockSpec((tm, tn), lambda i, j: (i, j))
    sem = ("arbitrary", "arbitrary")
    assert tn == 2 * RET_QK == RET_V == DIFF_W == d

    def w_spec(col0):
        return pl.BlockSpec((d, tn), lambda i, j: (0, col0 // tn + j))

    qk_r = pl.pallas_call(
        _proj_ret_kernel, grid=(n // tm, 1),
        in_specs=[h_spec, w_spec(OFF_QR), tab_spec, tab_spec],
        out_specs=o_spec, out_shape=jax.ShapeDtypeStruct((n, tn), BF16),
        compiler_params=_params(*sem), name="proj_ret",
    )(h16, w16, rcos, rsin)

    act_w = pl.BlockSpec((d, tn), lambda i, j: (0, jnp.where(j < 2, OFF_VR // tn + j, OFF_GA // tn + j - 2)))
    act = pl.pallas_call(
        _proj_act_kernel, grid=(n // tm, 4),
        in_specs=[h_spec, act_w],
        out_specs=o_spec, out_shape=jax.ShapeDtypeStruct((n, 4 * tn), BF16),
        compiler_params=_params(*sem), name="proj_act",
    )(h16, w16)

    qk_d, k32 = pl.pallas_call(
        _proj_diff_kernel, grid=(n // tm, 2),
        in_specs=[h_spec, w_spec(OFF_QD), tab_spec, tab_spec, tab_spec,
                  pl.BlockSpec((1, 1, MXU_N), lambda i, j: (j, 0, 0)),
                  pl.BlockSpec((MXU_N, MXU_N), lambda i, j: (0, 0))],
        out_specs=[o_spec, pl.BlockSpec((tm, tn), lambda i, j: (i, 0))],
        out_shape=[jax.ShapeDtypeStruct((n, 2 * tn), BF16), jax.ShapeDtypeStruct((n, tn), F32)],
        compiler_params=_params(*sem), name="proj_diff",
    )(h16, w16, dcos, dsa, dsb, qkg, gmat)

    v16, v32 = pl.pallas_call(
        _proj_val_kernel, grid=(n // tm, 1),
        in_specs=[h_spec, w_spec(OFF_VD)],
        out_specs=[o_spec, o_spec],
        out_shape=[jax.ShapeDtypeStruct((n, tn), BF16), jax.ShapeDtypeStruct((n, tn), F32)],
        compiler_params=_params(*sem), name="proj_val",
    )(h16, w16)
    return qk_r, act, qk_d, k32, v16, v32


def _ret_kernel(*refs, c_len, n_chunks, has_state):
    if has_state:
        (q_ref, k_ref, v_ref, g_ref, decay_ref, xi_ref, zeta_ref, glen_ref, gain_ref, s0_ref,
         o_ref, s_ref) = refs
        s = s0_ref[0, 0]
    else:
        (q_ref, k_ref, v_ref, g_ref, decay_ref, xi_ref, zeta_ref, glen_ref, gain_ref,
         o_ref, s_ref) = refs
        s = jnp.zeros((RET_DK, RET_DV), F32)
    decay = decay_ref[0]
    for c in range(n_chunks):
        rows = slice(c * c_len, (c + 1) * c_len)
        q = q_ref[rows, :]
        k = k_ref[rows, :]
        v = v_ref[rows, :]
        sc = _dot_nt(q, k) * decay
        o = _dot(sc.astype(BF16), v) + _dot(q, s.astype(BF16)) * xi_ref[0]
        kz = (k.astype(F32) * zeta_ref[0]).astype(BF16)
        s = s * glen_ref[0] + _dot_tn(kz, v)
        ms = jnp.mean(o * o, axis=-1, keepdims=True)
        on = o * lax.rsqrt(ms + EPS) * gain_ref[...]
        o_ref[rows, :] = (on * g_ref[rows, :].astype(F32)).astype(BF16)
    s_ref[0, 0] = s


def _retention(qk_r, act, tables, gain, s0, *, batch, t_len, c_len):
    n = qk_r.shape[0]
    decay, xi, zeta, glen = tables
    has_state = s0 is not None
    in_specs = [
        pl.BlockSpec((t_len, RET_DK), lambda b, h: (b, h)),
        pl.BlockSpec((t_len, RET_DK), lambda b, h: (b, N_RET_HEADS + h)),
        pl.BlockSpec((t_len, RET_DV), lambda b, h: (b, h)),
        pl.BlockSpec((t_len, RET_DV), lambda b, h: (b, N_RET_HEADS + h)),
        pl.BlockSpec((1, c_len, c_len), lambda b, h: (h, 0, 0)),
        pl.BlockSpec((1, c_len, RET_DV), lambda b, h: (h, 0, 0)),
        pl.BlockSpec((1, c_len, RET_DK), lambda b, h: (h, 0, 0)),
        pl.BlockSpec((1, 1, RET_DV), lambda b, h: (h, 0, 0)),
        pl.BlockSpec((1, RET_DV), lambda b, h: (0, 0)),
    ]
    args = [qk_r, qk_r, act, act, decay, xi, zeta, glen, gain]
    if has_state:
        in_specs.append(pl.BlockSpec((1, 1, RET_DK, RET_DV), lambda b, h: (b, h, 0, 0)))
        args.append(s0)
    return pl.pallas_call(
        functools.partial(_ret_kernel, c_len=c_len, n_chunks=t_len // c_len, has_state=has_state),
        grid=(batch, N_RET_HEADS),
        in_specs=in_specs,
        out_specs=[
            pl.BlockSpec((t_len, RET_DV), lambda b, h: (b, h)),
            pl.BlockSpec((1, 1, RET_DK, RET_DV), lambda b, h: (b, h, 0, 0)),
        ],
        out_shape=[
            jax.ShapeDtypeStruct((n, RET_V), BF16),
            jax.ShapeDtypeStruct((batch, N_RET_HEADS, RET_DK, RET_DV), F32),
        ],
        compiler_params=_params("arbitrary", "arbitrary"),
        name="ret",
    )(*args)


def _attn_kernel(q_ref, k_ref, v_ref, kc_ref, vc_ref, lam_ref, gain_ref, o_ref, vt_scr, s_scr, p_scr, *, blocks,
                 tb, causal, n_valid_last, lam_init):
    n_cache = 0 if kc_ref is None else kc_ref.shape[0]
    lk = n_cache + k_ref.shape[0]
    tq = blocks[0][1]

    def k_tile(t):
        r0 = t * tb
        if r0 < n_cache:
            return kc_ref[r0:r0 + tb, :].astype(BF16)
        return k_ref[r0 - n_cache:r0 - n_cache + tb, :]

    for t in range(lk // tb):
        r0 = t * tb
        v = vc_ref[r0:r0 + tb, :] if r0 < n_cache else v_ref[r0 - n_cache:r0 - n_cache + tb, :].astype(F32)
        vt_scr[:, r0:r0 + tb] = v.T.astype(BF16)
    lp = lam_ref[...]
    lam = (jnp.exp(jnp.sum(lp[0:1] * lp[1:2], axis=1, keepdims=True))
           - jnp.exp(jnp.sum(lp[2:3] * lp[3:4], axis=1, keepdims=True)) + lam_init)
    kidx = lax.broadcasted_iota(jnp.int32, (tb, tq), 0)
    if causal:
        qidx = lax.broadcasted_iota(jnp.int32, (tb, tq), 1)
        visible = (kidx // CHUNK) <= (qidx // CHUNK)
    else:
        visible = kidx < n_valid_last
    bias = jnp.where(visible, 0.0, NEG_BIG).astype(F32)

    def score_tiles(bi):
        q0, _, nk = blocks[bi]
        slot = bi % 2
        q = q_ref[q0:q0 + tq, :]
        lane = lax.broadcasted_iota(jnp.int32, q.shape, 1)
        qs = (jnp.where(lane < DIFF_HD, q, jnp.zeros_like(q)), jnp.where(lane >= DIFF_HD, q, jnp.zeros_like(q)))
        nt = nk // tb
        st = {"m": [None, None]}

        def tile(t):
            kt = k_tile(t)
            for c in range(2):
                s = _dot_nt(kt, qs[c])
                if t == nt - 1:
                    s = s + bias
                s_scr[slot, c, t] = s
                mt = jnp.max(s, axis=0, keepdims=True)
                st["m"][c] = mt if st["m"][c] is None else jnp.maximum(st["m"][c], mt)

        return [functools.partial(tile, t) for t in range(nt)], st

    def prob_tiles(bi, st1):
        q0, _, nk = blocks[bi]
        slot = bi % 2
        st = {"l": [None, None]}

        def tile(t):
            for c in range(2):
                p = jnp.exp2(s_scr[slot, c, t] - st1["m"][c])
                lt = jnp.sum(p, axis=0, keepdims=True)
                p_scr[slot, c, t * tb:(t + 1) * tb, :] = p.astype(BF16)
                st["l"][c] = lt if st["l"][c] is None else st["l"][c] + lt

        def finish():
            a0, a1 = (_dot(vt_scr[:, 0:nk], p_scr[slot, c, 0:nk, :]) for c in range(2))
            o = a0 * (1.0 / st["l"][0]) - (lam * (1.0 / st["l"][1])) * a1
            ms = jnp.mean(o * o, axis=0, keepdims=True)
            on = (o * lax.rsqrt(ms + EPS)).T
            o_ref[q0:q0 + tq, :] = (on * gain_ref[...] * (1.0 - lam_init)).astype(BF16)

        return [functools.partial(tile, t) for t in range(nk // tb)], finish

    nb = len(blocks)
    t1, st1 = score_tiles(0)
    for f in t1:
        f()
    for bi in range(nb):
        t2, finish = prob_tiles(bi, st1)
        t1n, st1n = score_tiles(bi + 1) if bi + 1 < nb else ([], None)
        for t in range(max(len(t2), len(t1n))):
            if t < len(t1n):
                t1n[t]()
            if t < len(t2):
                t2[t]()
        finish()
        st1 = st1n


def _attention(q_arr, k_arr, v_arr, kc_arr, vc_arr, lam_p, gain, *, batch, lq, lk, qcol, kcol, vcol, blocks, tb,
               causal, n_valid_last, lam_init):
    hd2 = 2 * DIFF_HD
    tq = blocks[0][1]
    has_cache = kc_arr is not None
    past = kc_arr.shape[0] // batch if has_cache else 0
    row = lambda col: (lambda b, h: (b, col + h))
    in_specs = [pl.BlockSpec((lq, hd2), row(qcol)), pl.BlockSpec((lk, hd2), row(kcol)),
                pl.BlockSpec((lk, hd2), row(vcol))]
    args = [q_arr, k_arr, v_arr]
    if has_cache:
        in_specs += [pl.BlockSpec((past, hd2), row(0)), pl.BlockSpec((past, hd2), row(0))]
        args += [kc_arr, vc_arr]
    in_specs += [pl.BlockSpec((4, DIFF_HD), lambda b, h: (0, 0)), pl.BlockSpec((1, hd2), lambda b, h: (0, 0))]
    args += [lam_p, gain]

    def body(*refs):
        refs = list(refs)
        q_ref, k_ref, v_ref = refs[:3]
        kc_ref, vc_ref = (refs[3], refs[4]) if has_cache else (None, None)
        lam_ref, gain_ref, o_ref, vt_scr, s_scr, p_scr = refs[5:] if has_cache else refs[3:]
        _attn_kernel(q_ref, k_ref, v_ref, kc_ref, vc_ref, lam_ref, gain_ref, o_ref, vt_scr, s_scr, p_scr,
                     blocks=blocks, tb=tb, causal=causal, n_valid_last=n_valid_last, lam_init=lam_init)

    lk_all = past + lk
    return pl.pallas_call(
        body,
        grid=(batch, N_DIFF_HEADS),
        in_specs=in_specs,
        out_specs=pl.BlockSpec((lq, hd2), lambda b, h: (b, h)),
        out_shape=jax.ShapeDtypeStruct((batch * lq, DIFF_W), BF16),
        scratch_shapes=[pltpu.VMEM((hd2, lk_all), BF16),
                        pltpu.VMEM((2, 2, lk_all // tb, tb, tq), F32),
                        pltpu.VMEM((2, 2, lk_all, tq), BF16)],
        compiler_params=_params("arbitrary", "arbitrary"),
        name="attn",
    )(*args)


def _merge_kernel(ret_ref, dif_ref, sa_ref, sb_ref, x_ref, wa_ref, wb_ref, wo_ref, g2_ref, x1_ref, h2_ref):
    ya = _dot(ret_ref[...], wa_ref[...])
    yb = _dot(dif_ref[...], wb_ref[...])
    y = sa_ref[...].astype(F32) * ya + sb_ref[...].astype(F32) * yb
    x1 = x_ref[...] + _dot(y.astype(BF16), wo_ref[...])
    x1_ref[...] = x1
    h2_ref[...] = _rms(x1, g2_ref[...]).astype(BF16)


def _merge(ret16, dif16, act, x, wa16, wb16, wo16, g2, *, tm):
    n, d = x.shape
    full = lambda i: (0, 0)
    return pl.pallas_call(
        _merge_kernel,
        grid=(n // tm,),
        in_specs=[
            pl.BlockSpec((tm, RET_V), lambda i: (i, 0)),
            pl.BlockSpec((tm, DIFF_W), lambda i: (i, 0)),
            pl.BlockSpec((tm, d), lambda i: (i, 2)),
            pl.BlockSpec((tm, d), lambda i: (i, 3)),
            pl.BlockSpec((tm, d), lambda i: (i, 0)),
            pl.BlockSpec((RET_V, d), full),
            pl.BlockSpec((DIFF_W, d), full),
            pl.BlockSpec((d, d), full),
            pl.BlockSpec((1, d), full),
        ],
        out_specs=[pl.BlockSpec((tm, d), lambda i: (i, 0)), pl.BlockSpec((tm, d), lambda i: (i, 0))],
        out_shape=[jax.ShapeDtypeStruct((n, d), F32), jax.ShapeDtypeStruct((n, d), BF16)],
        compiler_params=_params("arbitrary"),
        name="merge",
    )(ret16, dif16, act, act, x, wa16, wb16, wo16, g2)


def _glu_kernel(*refs, tm, n_t, d_ff, cw, has_state, has_next):
    refs = list(refs)
    h_ref, x1_ref, wu_ref, wd_ref, cwt_ref, cb_ref = refs[:6]
    del refs[:6]
    st_ref = refs.pop(0) if has_state else None
    gn_ref = refs.pop(0) if has_next else None
    y_ref, cs_ref = refs[:2]
    del refs[:2]
    hn_ref = refs.pop(0) if has_next else None
    (carry,) = refs
    ti = pl.program_id(0) % n_t

    @pl.when(ti == 0)
    def _():
        carry[...] = jnp.zeros_like(carry)
        if has_state:
            carry[6:8, :] = st_ref[0]

    h = h_ref[...]
    row8 = lax.broadcasted_iota(jnp.int32, (8, cw), 0)

    def up(c):
        return [_dot(h, wu_ref[:, base:base + cw]) for base in (c * cw, d_ff + c * cw)]

    def rest(c, us, acc):
        conv = []
        for u, base in zip(us, (c * cw, d_ff + c * cw)):
            cols = slice(base, base + cw)
            c6 = jnp.broadcast_to(carry[6:7, cols], (8, cw))
            c7 = jnp.broadcast_to(carry[7:8, cols], (8, cw))
            carry[6:8, cols] = u[tm - 2:tm, :]
            r1 = pltpu.roll(u, 1, 0)
            r2 = pltpu.roll(u, 2, 0)
            top1 = jnp.where(row8 == 0, c7, r1[0:8])
            top2 = jnp.where(row8 == 0, c6, jnp.where(row8 == 1, c7, r2[0:8]))
            if tm > 8:
                sh1 = jnp.concatenate([top1, r1[8:]], axis=0)
                sh2 = jnp.concatenate([top2, r2[8:]], axis=0)
            else:
                sh1, sh2 = top1, top2
            r = cb_ref[:, cols] + cwt_ref[0:1, cols] * sh2
            r = r + cwt_ref[1:2, cols] * sh1
            r = r + cwt_ref[2:3, cols] * u
            conv.append(r)
        a, g = conv
        act = (g * _sigmoid(g) * a).astype(BF16)
        dn = _dot(act, wd_ref[c * cw:(c + 1) * cw, :])
        return dn if acc is None else acc + dn

    n_c = d_ff // cw
    acc = None
    us = up(0)
    for c in range(n_c):
        nxt = up(c + 1) if c + 1 < n_c else None
        acc = rest(c, us, acc)
        us = nxt
    y = x1_ref[...] + acc
    y_ref[...] = y
    if has_next:
        hn_ref[...] = _rms(y, gn_ref[...]).astype(BF16)

    @pl.when(ti == n_t - 1)
    def _():
        cs_ref[0] = carry[6:8, :]


def _glu(h2, x1, wu16, wd16, conv_w, conv_b, state, g_next, *, batch, t_len, tm):
    n, d = x1.shape
    d_ff = wd16.shape[0]
    n_t = t_len // tm
    has_state = state is not None
    has_next = g_next is not None
    full = lambda i: (0, 0)
    row = pl.BlockSpec((tm, d), lambda i: (i, 0))
    in_specs = [row, row, pl.BlockSpec((d, 2 * d_ff), full), pl.BlockSpec((d_ff, d), full),
                pl.BlockSpec((CONV_W, 2 * d_ff), full), pl.BlockSpec((1, 2 * d_ff), full)]
    args = [h2, x1, wu16, wd16, conv_w, conv_b]
    st_spec = pl.BlockSpec((1, CONV_W - 1, 2 * d_ff), lambda i: (i // n_t, 0, 0))
    if has_state:
        in_specs.append(st_spec)
        args.append(state)
    out_specs = [row, st_spec]
    out_shape = [jax.ShapeDtypeStruct((n, d), F32), jax.ShapeDtypeStruct((batch, CONV_W - 1, 2 * d_ff), F32)]
    if has_next:
        in_specs.append(pl.BlockSpec((1, d), full))
        args.append(g_next)
        out_specs.append(row)
        out_shape.append(jax.ShapeDtypeStruct((n, d), BF16))
    res = pl.pallas_call(
        functools.partial(_glu_kernel, tm=tm, n_t=n_t, d_ff=d_ff, cw=MXU_N, has_state=has_state,
                          has_next=has_next),
        grid=(n // tm,),
        in_specs=in_specs,
        out_specs=out_specs,
        out_shape=out_shape,
        scratch_shapes=[pltpu.VMEM((8, 2 * d_ff), F32)],
        compiler_params=_params("arbitrary"),
        name="glu",
    )(*args)
    return res if has_next else (*res, None)


def _rope_tables(pos, reps):
    posf = pos.astype(F32)[:, None]
    lane = np.arange(LANES)
    half = RET_DK // 2
    inv = RET_ROPE_THETA ** (-jnp.arange(half, dtype=F32) / half)
    ang = posf * inv[None, :]
    cos, sin = jnp.cos(ang), jnp.sin(ang)
    rcos = jnp.concatenate([cos, cos], axis=1)
    rsin = jnp.concatenate([-sin, sin], axis=1)
    half = DIFF_ROT // 2
    inv = ROPE_THETA ** (-jnp.arange(half, dtype=F32) / half)
    ang = posf * inv[None, :]
    cos, sin = jnp.cos(ang), jnp.sin(ang)
    grp = lane % DIFF_HD
    sel = jnp.asarray(grp % half)
    cos_l, sin_l = cos[:, sel], sin[:, sel]
    dcos = jnp.where(jnp.asarray(grp < DIFF_ROT)[None, :], cos_l, 1.0)
    dsa = jnp.where(jnp.asarray((grp >= half) & (grp < DIFF_ROT))[None, :], sin_l, 0.0)
    dsb = jnp.where(jnp.asarray(grp < half)[None, :], -sin_l, 0.0)
    return tuple(jnp.tile(t.astype(F32), (reps, 1)) for t in (rcos, rsin, dcos, dsa, dsb))


def _ret_tables(c_len):
    log_g = jnp.log1p(-jnp.exp2(-5.0 - jnp.arange(N_RET_HEADS, dtype=F32)))
    idx = jnp.arange(c_len, dtype=F32)
    rel = idx[:, None] - idx[None, :]
    decay = jnp.where(rel >= 0, jnp.exp(log_g[:, None, None] * jnp.maximum(rel, 0.0)), 0.0)
    xi = jnp.exp(log_g[:, None] * (idx[None, :] + 1.0))
    zeta = jnp.exp(log_g[:, None] * (c_len - 1.0 - idx[None, :]))
    glen = jnp.exp(log_g * c_len)
    return (decay.astype(F32),
            jnp.broadcast_to(xi[:, :, None], (N_RET_HEADS, c_len, RET_DV)).astype(F32),
            jnp.broadcast_to(zeta[:, :, None], (N_RET_HEADS, c_len, RET_DK)).astype(F32),
            jnp.broadcast_to(glen[:, None, None], (N_RET_HEADS, 1, RET_DV)).astype(F32))


def _pick_tile(n, target):
    t = min(n, target)
    while n % t:
        t //= 2
    return t


def kernel(x_prompt, x_sample, cache_k, cache_v, state_ret, state_conv, norm1_g, w_in, ret_norm_g, q_norm_g,
           k_norm_g, lambda_q1, lambda_k1, lambda_q2, lambda_k2, diff_norm_g, w_branch_a, w_branch_b, w_out,
           norm2_g, w_up, conv_w, conv_b, w_down):
    bp, tp, d = x_prompt.shape
    bs, ls, _ = x_sample.shape
    depth = w_in.shape[0]
    past = cache_k.shape[2]
    d_ff = w_down.shape[1]
    hd2 = 2 * DIFF_HD

    tm_p = _pick_tile(tp, 1024)
    tn = d
    c_p = _pick_tile(tp, 256)
    tq_p = _pick_tile(tp, 256)
    tm_glu = _pick_tile(tp, 512)
    ns = bs * ls
    tq_s = LANES
    assert past % LANES == 0 and ls <= tq_s
    blocks_p = tuple((i * tq_p, tq_p, (i + 1) * tq_p) for i in range(tp // tq_p))
    blocks_s = ((0, tq_s, past + tq_s),)

    pos_p = jnp.arange(tp, dtype=jnp.int32)
    pos_s = past + jnp.arange(ls, dtype=jnp.int32)
    tabs_p = _rope_tables(pos_p, 1)
    tabs_s = _rope_tables(pos_s, bs)
    rt_p = _ret_tables(c_p)
    rt_s = _ret_tables(ls)
    blk = np.arange(MXU_N) // DIFF_HD
    gmat = jnp.asarray(blk[:, None] == blk[None, :], dtype=BF16)

    yp = x_prompt.reshape(bp * tp, d)
    ys = x_sample.reshape(ns, d)
    hp = _norm(yp, norm1_g[0].reshape(1, d), tm=tm_p)
    hs = _norm(ys, norm1_g[0].reshape(1, d), tm=ns)
    outs = {k: [] for k in ("kp", "vp", "sp", "cp", "ks", "vs", "ss", "cs")}
    for l in range(depth):
        lam_init = 0.8 - 0.6 * math.exp(-0.3 * l)
        g2 = norm2_g[l].reshape(1, d)
        g_next = norm1_g[l + 1].reshape(1, d) if l + 1 < depth else None
        w16 = w_in[l].astype(BF16)
        wa16 = w_branch_a[l].astype(BF16)
        wb16 = w_branch_b[l].astype(BF16)
        wo16 = w_out[l].astype(BF16)
        wu16 = w_up[l].astype(BF16)
        wd16 = w_down[l].astype(BF16)
        reps = MXU_N // DIFF_HD
        qkg = jnp.stack([jnp.tile(q_norm_g[l], reps), jnp.tile(k_norm_g[l], reps)]).reshape(2, 1, MXU_N)
        rgain = ret_norm_g[l].reshape(1, RET_DV)
        dgain = diff_norm_g[l].reshape(1, hd2)
        lam_p = jnp.stack([lambda_q1[l], lambda_k1[l], lambda_q2[l], lambda_k2[l]])
        cw_l = conv_w[l]
        cb_l = conv_b[l].reshape(1, 2 * d_ff)

        qk_r, act, qk_d, k32, v16, v32 = _proj(hp, w16, tabs_p, qkg, gmat, tm=tm_p, tn=tn)
        ret16, s_fin = _retention(qk_r, act, rt_p, rgain, None, batch=bp, t_len=tp, c_len=c_p)
        dif16 = _attention(qk_d, qk_d, v16, None, None, lam_p, dgain, batch=bp, lq=tp, lk=tp, qcol=0,
                           kcol=N_DIFF_HEADS, vcol=0, blocks=blocks_p, tb=tq_p, causal=True,
                           n_valid_last=0, lam_init=lam_init)
        x1, h2 = _merge(ret16, dif16, act, yp, wa16, wb16, wo16, g2, tm=tm_p)
        yp, cst, hp = _glu(h2, x1, wu16, wd16, cw_l, cb_l, None, g_next, batch=bp, t_len=tp, tm=tm_glu)
        outs["kp"].append(k32.reshape(bp, tp, N_DIFF_HEADS, hd2))
        outs["vp"].append(v32.reshape(bp, tp, N_DIFF_HEADS, hd2))
        outs["sp"].append(s_fin)
        outs["cp"].append(cst)

        qk_r, act, qk_d, k32, v16, v32 = _proj(hs, w16, tabs_s, qkg, gmat, tm=ns, tn=tn)
        ret16, s_fin = _retention(qk_r, act, rt_s, rgain, state_ret[l], batch=bs, t_len=ls, c_len=ls)
        pad_rows = lambda a: jnp.pad(a.reshape(bs, ls, DIFF_W), ((0, 0), (0, tq_s - ls), (0, 0))).reshape(
            bs * tq_s, DIFF_W)
        dif_pad = _attention(pad_rows(qk_d[:, :DIFF_W]), pad_rows(qk_d[:, DIFF_W:]), pad_rows(v16),
                             cache_k[l].reshape(bs * past, DIFF_W), cache_v[l].reshape(bs * past, DIFF_W),
                             lam_p, dgain, batch=bs, lq=tq_s, lk=tq_s, qcol=0, kcol=0, vcol=0,
                             blocks=blocks_s, tb=LANES, causal=False, n_valid_last=ls, lam_init=lam_init)
        dif16 = dif_pad.reshape(bs, tq_s, DIFF_W)[:, :ls].reshape(ns, DIFF_W)
        x1, h2 = _merge(ret16, dif16, act, ys, wa16, wb16, wo16, g2, tm=ns)
        ys, cst, hs = _glu(h2, x1, wu16, wd16, cw_l, cb_l, state_conv[l], g_next, batch=bs, t_len=ls, tm=ls)
        outs["ks"].append(k32.reshape(bs, ls, N_DIFF_HEADS, hd2))
        outs["vs"].append(v32.reshape(bs, ls, N_DIFF_HEADS, hd2))
        outs["ss"].append(s_fin)
        outs["cs"].append(cst)

    st = {k: jnp.stack(v) for k, v in outs.items()}
    return (yp.reshape(bp, tp, d), ys.reshape(bs, ls, d), st["kp"], st["vp"], st["sp"], st["cp"],
            st["ks"], st["vs"], st["ss"], st["cs"])
```

```python
import functools
import math

import jax
import jax.numpy as jnp
import numpy as np
from jax import lax
from jax.experimental import pallas as pl
from jax.experimental.pallas import tpu as pltpu

F32 = jnp.float32
BF16 = jnp.bfloat16

EPS = 1e-6
CHUNK = 64
N_RET_HEADS = 4
RET_DK = 128
RET_DV = 256
RET_ROPE_THETA = 10000.0
N_DIFF_HEADS = 8
DIFF_HD = 64
DIFF_ROT = DIFF_HD // 4
ROPE_THETA = 500000.0
CONV_W = 3

RET_QK = N_RET_HEADS * RET_DK
RET_V = N_RET_HEADS * RET_DV
DIFF_W = N_DIFF_HEADS * 2 * DIFF_HD
OFF_QR = 0
OFF_KR = OFF_QR + RET_QK
OFF_VR = OFF_KR + RET_QK
OFF_GR = OFF_VR + RET_V
OFF_QD = OFF_GR + RET_V
OFF_KD = OFF_QD + DIFF_W
OFF_VD = OFF_KD + DIFF_W
OFF_GA = OFF_VD + DIFF_W

LANES = 128
MXU_N = 256
VMEM_LIMIT = 56 * 1024 * 1024
NEG_BIG = -1e30
LOG2E = math.log2(math.e)


def _sigmoid(x):
    return 1.0 / (1.0 + jnp.exp(-x))


def _dot(a, b):
    return jnp.dot(a, b, preferred_element_type=F32)


def _dot_nt(a, b):
    return lax.dot_general(a, b, (((1,), (1,)), ((), ())), preferred_element_type=F32)


def _dot_tn(a, b):
    return lax.dot_general(a, b, (((0,), (0,)), ((), ())), preferred_element_type=F32)


def _params(*sem):
    return pltpu.CompilerParams(dimension_semantics=sem, vmem_limit_bytes=VMEM_LIMIT)


def _rms(x, g):
    ms = jnp.mean(x * x, axis=-1, keepdims=True)
    return x * lax.rsqrt(ms + EPS) * g


def _norm_kernel(x_ref, g_ref, h_ref):
    h_ref[...] = _rms(x_ref[...], g_ref[...]).astype(BF16)


def _norm(x, g, *, tm):
    n, d = x.shape
    return pl.pallas_call(
        _norm_kernel,
        grid=(n // tm,),
        in_specs=[pl.BlockSpec((tm, d), lambda i: (i, 0)), pl.BlockSpec((1, d), lambda i: (0, 0))],
        out_specs=pl.BlockSpec((tm, d), lambda i: (i, 0)),
        out_shape=jax.ShapeDtypeStruct((n, d), BF16),
        compiler_params=_params("arbitrary"),
        name="norm",
    )(x, g)


def _proj_ret_kernel(h_ref, w_ref, rcos_ref, rsin_ref, o_ref):
    y = _dot(h_ref[...], w_ref[...])
    for c in range(y.shape[1] // LANES):
        ys = y[:, c * LANES:(c + 1) * LANES]
        r = ys * rcos_ref[...] + pltpu.roll(ys, RET_DK // 2, 1) * rsin_ref[...]
        if c * LANES >= RET_QK:
            r = r * (RET_DK ** -0.5)
        o_ref[:, c * LANES:(c + 1) * LANES] = r.astype(BF16)


def _proj_act_kernel(h_ref, w_ref, o_ref):
    j = pl.program_id(1)
    y = _dot(h_ref[...], w_ref[...])
    sg = _sigmoid(y)
    o_ref[...] = jnp.where(j == 0, y, jnp.where(j == 1, y * sg, sg)).astype(BF16)


def _proj_diff_kernel(h_ref, w_ref, dcos_ref, dsa_ref, dsb_ref, gain_ref, gmat_ref, o16_ref, o32_ref):
    j = pl.program_id(1)
    scale = jnp.where(j == 0, DIFF_HD ** -0.5 * LOG2E, 1.0).astype(F32)
    h = h_ref[...]
    n_c = w_ref.shape[1] // MXU_N

    def slab(c):
        return _dot(h, w_ref[:, c * MXU_N:(c + 1) * MXU_N])

    nxt = slab(0)
    for c in range(n_c):
        ys, nxt = nxt, (slab(c + 1) if c + 1 < n_c else None)
        ss = _dot((ys * ys).astype(BF16), gmat_ref[...])
        yn = ys * lax.rsqrt(ss * (1.0 / DIFF_HD) + EPS) * gain_ref[0]
        for d in range(MXU_N // LANES):
            z = yn[:, d * LANES:(d + 1) * LANES]
            r = (z * dcos_ref[...] + pltpu.roll(z, DIFF_ROT // 2, 1) * dsa_ref[...]
                 + pltpu.roll(z, LANES - DIFF_ROT // 2, 1) * dsb_ref[...])
            lo = c * MXU_N + d * LANES
            o32_ref[:, lo:lo + LANES] = r
            o16_ref[:, lo:lo + LANES] = (r * scale).astype(BF16)


def _proj_val_kernel(h_ref, w_ref, o16_ref, o32_ref):
    y = _dot(h_ref[...], w_ref[...])
    o32_ref[...] = y
    o16_ref[...] = y.astype(BF16)


def _proj(h16, w16, tabs, qkg, gmat, *, tm, tn):
    n, d = h16.shape
    rcos, rsin, dcos, dsa, dsb = tabs
    nt = rcos.shape[0] // tm
    h_spec = pl.BlockSpec((tm, d), lambda i, j: (i, 0))
    tab_spec = pl.BlockSpec((tm, LANES), lambda i, j: (i % nt, 0))
    o_spec = pl.BlockSpec((tm, tn), lambda i, j: (i, j))
    sem = ("arbitrary", "arbitrary")
    assert tn == 2 * RET_QK == RET_V == DIFF_W == d

    def w_spec(col0):
        return pl.BlockSpec((d, tn), lambda i, j: (0, col0 // tn + j))

    qk_r = pl.pallas_call(
        _proj_ret_kernel, grid=(n // tm, 1),
        in_specs=[h_spec, w_spec(OFF_QR), tab_spec, tab_spec],
        out_specs=o_spec, out_shape=jax.ShapeDtypeStruct((n, tn), BF16),
        compiler_params=_params(*sem), name="proj_ret",
    )(h16, w16, rcos, rsin)

    act_w = pl.BlockSpec((d, tn), lambda i, j: (0, jnp.where(j < 2, OFF_VR // tn + j, OFF_GA // tn + j - 2)))
    act = pl.pallas_call(
        _proj_act_kernel, grid=(n // tm, 4),
        in_specs=[h_spec, act_w],
        out_specs=o_spec, out_shape=jax.ShapeDtypeStruct((n, 4 * tn), BF16),
        compiler_params=_params(*sem), name="proj_act",
    )(h16, w16)

    qk_d, k32 = pl.pallas_call(
        _proj_diff_kernel, grid=(n // tm, 2),
        in_specs=[h_spec, w_spec(OFF_QD), tab_spec, tab_spec, tab_spec,
                  pl.BlockSpec((1, 1, MXU_N), lambda i, j: (j, 0, 0)),
                  pl.BlockSpec((MXU_N, MXU_N), lambda i, j: (0, 0))],
        out_specs=[o_spec, pl.BlockSpec((tm, tn), lambda i, j: (i, 0))],
        out_shape=[jax.ShapeDtypeStruct((n, 2 * tn), BF16), jax.ShapeDtypeStruct((n, tn), F32)],
        compiler_params=_params(*sem), name="proj_diff",
    )(h16, w16, dcos, dsa, dsb, qkg, gmat)

    v16, v32 = pl.pallas_call(
        _proj_val_kernel, grid=(n // tm, 1),
        in_specs=[h_spec, w_spec(OFF_VD)],
        out_specs=[o_spec, o_spec],
        out_shape=[jax.ShapeDtypeStruct((n, tn), BF16), jax.ShapeDtypeStruct((n, tn), F32)],
        compiler_params=_params(*sem), name="proj_val",
    )(h16, w16)
    return qk_r, act, qk_d, k32, v16, v32


def _ret_kernel(*refs, c_len, n_chunks, has_state):
    if has_state:
        (q_ref, k_ref, v_ref, g_ref, decay_ref, xi_ref, zeta_ref, glen_ref, gain_ref, s0_ref,
         o_ref, s_ref) = refs
        s = s0_ref[0, 0]
    else:
        (q_ref, k_ref, v_ref, g_ref, decay_ref, xi_ref, zeta_ref, glen_ref, gain_ref,
         o_ref, s_ref) = refs
        s = jnp.zeros((RET_DK, RET_DV), F32)
    decay = decay_ref[0]
    for c in range(n_chunks):
        rows = slice(c * c_len, (c + 1) * c_len)
        q = q_ref[rows, :]
        k = k_ref[rows, :]
        v = v_ref[rows, :]
        sc = _dot_nt(q, k) * decay
        o = _dot(sc.astype(BF16), v) + _dot(q, s.astype(BF16)) * xi_ref[0]
        kz = (k.astype(F32) * zeta_ref[0]).astype(BF16)
        s = s * glen_ref[0] + _dot_tn(kz, v)
        ms = jnp.mean(o * o, axis=-1, keepdims=True)
        on = o * lax.rsqrt(ms + EPS) * gain_ref[...]
        o_ref[rows, :] = (on * g_ref[rows, :].astype(F32)).astype(BF16)
    s_ref[0, 0] = s


def _retention(qk_r, act, tables, gain, s0, *, batch, t_len, c_len):
    n = qk_r.shape[0]
    decay, xi, zeta, glen = tables
    has_state = s0 is not None
    in_specs = [
        pl.BlockSpec((t_len, RET_DK), lambda b, h: (b, h)),
        pl.BlockSpec((t_len, RET_DK), lambda b, h: (b, N_RET_HEADS + h)),
        pl.BlockSpec((t_len, RET_DV), lambda b, h: (b, h)),
        pl.BlockSpec((t_len, RET_DV), lambda b, h: (b, N_RET_HEADS + h)),
        pl.BlockSpec((1, c_len, c_len), lambda b, h: (h, 0, 0)),
        pl.BlockSpec((1, c_len, RET_DV), lambda b, h: (h, 0, 0)),
        pl.BlockSpec((1, c_len, RET_DK), lambda b, h: (h, 0, 0)),
        pl.BlockSpec((1, 1, RET_DV), lambda b, h: (h, 0, 0)),
        pl.BlockSpec((1, RET_DV), lambda b, h: (0, 0)),
    ]
    args = [qk_r, qk_r, act, act, decay, xi, zeta, glen, gain]
    if has_state:
        in_specs.append(pl.BlockSpec((1, 1, RET_DK, RET_DV), lambda b, h: (b, h, 0, 0)))
        args.append(s0)
    return pl.pallas_call(
        functools.partial(_ret_kernel, c_len=c_len, n_chunks=t_len // c_len, has_state=has_state),
        grid=(batch, N_RET_HEADS),
        in_specs=in_specs,
        out_specs=[
            pl.BlockSpec((t_len, RET_DV), lambda b, h: (b, h)),
            pl.BlockSpec((1, 1, RET_DK, RET_DV), lambda b, h: (b, h, 0, 0)),
        ],
        out_shape=[
            jax.ShapeDtypeStruct((n, RET_V), BF16),
            jax.ShapeDtypeStruct((batch, N_RET_HEADS, RET_DK, RET_DV), F32),
        ],
        compiler_params=_params("arbitrary", "arbitrary"),
        name="ret",
    )(*args)


def _attn_kernel(q_ref, k_ref, v_ref, kc_ref, vc_ref, lam_ref, gain_ref, o_ref, vt_scr, s_scr, p_scr, *, blocks,
                 tb, causal, n_valid_last, lam_init, head=None):
    cs = slice(None)
    n_cache = 0 if kc_ref is None else kc_ref.shape[0]
    lk = n_cache + k_ref.shape[0]
    tq = blocks[0][1]

    def cache_rows(ref, r0):
        return ref[r0:r0 + tb, :] if head is None else ref[r0:r0 + tb, head, :]

    def k_tile(t):
        r0 = t * tb
        if r0 < n_cache:
            return cache_rows(kc_ref, r0).astype(BF16)
        return k_ref[r0 - n_cache:r0 - n_cache + tb, cs]

    for t in range(lk // tb):
        r0 = t * tb
        v = cache_rows(vc_ref, r0) if r0 < n_cache else v_ref[r0 - n_cache:r0 - n_cache + tb, cs].astype(F32)
        vt_scr[:, r0:r0 + tb] = v.T.astype(BF16)
    lp = lam_ref[...]
    lam = (jnp.exp(jnp.sum(lp[0:1] * lp[1:2], axis=1, keepdims=True))
           - jnp.exp(jnp.sum(lp[2:3] * lp[3:4], axis=1, keepdims=True)) + lam_init)
    kidx = lax.broadcasted_iota(jnp.int32, (tb, tq), 0)
    if causal:
        qidx = lax.broadcasted_iota(jnp.int32, (tb, tq), 1)
        visible = (kidx // CHUNK) <= (qidx // CHUNK)
    else:
        visible = kidx < n_valid_last
    bias = jnp.where(visible, 0.0, NEG_BIG).astype(F32)

    def score_tiles(bi):
        q0, _, nk = blocks[bi]
        slot = bi % 2
        q = q_ref[q0:q0 + tq, cs]
        lane = lax.broadcasted_iota(jnp.int32, q.shape, 1)
        qs = (jnp.where(lane < DIFF_HD, q, jnp.zeros_like(q)), jnp.where(lane >= DIFF_HD, q, jnp.zeros_like(q)))
        nt = nk // tb
        st = {"m": [None, None]}

        def tile(t):
            kt = k_tile(t)
            for c in range(2):
                s = _dot_nt(kt, qs[c])
                if t == nt - 1:
                    s = s + bias
                s_scr[slot, c, t] = s
                mt = jnp.max(s, axis=0, keepdims=True)
                st["m"][c] = mt if st["m"][c] is None else jnp.maximum(st["m"][c], mt)

        return [functools.partial(tile, t) for t in range(nt)], st

    def prob_tiles(bi, st1):
        q0, _, nk = blocks[bi]
        slot = bi % 2
        st = {"l": [None, None]}

        def tile(t):
            for c in range(2):
                p = jnp.exp2(s_scr[slot, c, t] - st1["m"][c])
                lt = jnp.sum(p, axis=0, keepdims=True)
                p_scr[slot, c, t * tb:(t + 1) * tb, :] = p.astype(BF16)
                st["l"][c] = lt if st["l"][c] is None else st["l"][c] + lt

        def finish():
            a0, a1 = (_dot(vt_scr[:, 0:nk], p_scr[slot, c, 0:nk, :]) for c in range(2))
            o = a0 * (1.0 / st["l"][0]) - (lam * (1.0 / st["l"][1])) * a1
            ms = jnp.mean(o * o, axis=0, keepdims=True)
            on = (o * lax.rsqrt(ms + EPS)).T
            o_ref[q0:q0 + tq, cs] = (on * gain_ref[...] * (1.0 - lam_init)).astype(BF16)

        return [functools.partial(tile, t) for t in range(nk // tb)], finish

    nb = len(blocks)
    t1, st1 = score_tiles(0)
    for f in t1:
        f()
    for bi in range(nb):
        t2, finish = prob_tiles(bi, st1)
        t1n, st1n = score_tiles(bi + 1) if bi + 1 < nb else ([], None)
        for t in range(max(len(t2), len(t1n))):
            if t < len(t1n):
                t1n[t]()
            if t < len(t2):
                t2[t]()
        finish()
        st1 = st1n


def _attention(q_arr, k_arr, v_arr, kc_arr, vc_arr, lam_p, gain, *, batch, lq, lk, qcol, kcol, vcol, blocks, tb,
               causal, n_valid_last, lam_init, layer=0):
    hd2 = 2 * DIFF_HD
    tq = blocks[0][1]
    has_cache = kc_arr is not None
    past = kc_arr.shape[2] if has_cache else 0
    lk_all = past + lk
    scratch = [pltpu.VMEM((hd2, lk_all), BF16),
               pltpu.VMEM((2, 2, lk_all // tb, tb, tq), F32),
               pltpu.VMEM((2, 2, lk_all, tq), BF16)]
    static = dict(blocks=blocks, tb=tb, causal=causal, n_valid_last=n_valid_last, lam_init=lam_init)
    small = lambda shape: pl.BlockSpec(shape, lambda *_: (0, 0))
    out_shape = jax.ShapeDtypeStruct((batch * lq, DIFF_W), BF16)

    if not has_cache:
        def body(q_ref, k_ref, v_ref, lam_ref, gain_ref, o_ref, vt_scr, s_scr, p_scr):
            _attn_kernel(q_ref, k_ref, v_ref, None, None, lam_ref, gain_ref, o_ref, vt_scr, s_scr, p_scr, **static)

        row = lambda col: (lambda b, h: (b, col + h))
        return pl.pallas_call(
            body,
            grid=(batch, N_DIFF_HEADS),
            in_specs=[pl.BlockSpec((lq, hd2), row(qcol)), pl.BlockSpec((lk, hd2), row(kcol)),
                      pl.BlockSpec((lk, hd2), row(vcol)), small((4, DIFF_HD)), small((1, hd2))],
            out_specs=pl.BlockSpec((lq, hd2), lambda b, h: (b, h)),
            out_shape=out_shape,
            scratch_shapes=scratch,
            compiler_params=_params("arbitrary", "arbitrary"),
            name="attn",
        )(q_arr, k_arr, v_arr, lam_p, gain)

    def body_cache(q_ref, k_ref, v_ref, kc_ref, vc_ref, lam_ref, gain_ref, o_ref, vt_scr, s_scr, p_scr):
        for h in range(N_DIFF_HEADS):
            @pl.when(pl.program_id(1) == h)
            def _(h=h):
                _attn_kernel(q_ref, k_ref, v_ref, kc_ref, vc_ref, lam_ref, gain_ref, o_ref, vt_scr, s_scr,
                             p_scr, head=h, **static)

    row = lambda col: (lambda b, h: (b, col + h))
    cache_spec = pl.BlockSpec((None, None, past, N_DIFF_HEADS, hd2), lambda b, h: (layer, b, 0, 0, 0))
    return pl.pallas_call(
        body_cache,
        grid=(batch, N_DIFF_HEADS),
        in_specs=[pl.BlockSpec((lq, hd2), row(qcol)), pl.BlockSpec((lk, hd2), row(kcol)),
                  pl.BlockSpec((lk, hd2), row(vcol)), cache_spec, cache_spec, small((4, DIFF_HD)),
                  small((1, hd2))],
        out_specs=pl.BlockSpec((lq, hd2), lambda b, h: (b, h)),
        out_shape=out_shape,
        scratch_shapes=scratch,
        compiler_params=_params("arbitrary", "arbitrary"),
        name="attn_cache",
    )(q_arr, k_arr, v_arr, kc_arr, vc_arr, lam_p, gain)


def _merge_kernel(ret_ref, dif_ref, sa_ref, sb_ref, x_ref, wa_ref, wb_ref, wo_ref, g2_ref, x1_ref, h2_ref):
    ya = _dot(ret_ref[...], wa_ref[...])
    yb = _dot(dif_ref[...], wb_ref[...])
    y = sa_ref[...].astype(F32) * ya + sb_ref[...].astype(F32) * yb
    x1 = x_ref[...] + _dot(y.astype(BF16), wo_ref[...])
    x1_ref[...] = x1
    h2_ref[...] = _rms(x1, g2_ref[...]).astype(BF16)


def _merge(ret16, dif16, act, x, wa16, wb16, wo16, g2, *, tm):
    n, d = x.shape
    full = lambda i: (0, 0)
    return pl.pallas_call(
        _merge_kernel,
        grid=(n // tm,),
        in_specs=[
            pl.BlockSpec((tm, RET_V), lambda i: (i, 0)),
            pl.BlockSpec((tm, DIFF_W), lambda i: (i, 0)),
            pl.BlockSpec((tm, d), lambda i: (i, 2)),
            pl.BlockSpec((tm, d), lambda i: (i, 3)),
            pl.BlockSpec((tm, d), lambda i: (i, 0)),
            pl.BlockSpec((RET_V, d), full),
            pl.BlockSpec((DIFF_W, d), full),
            pl.BlockSpec((d, d), full),
            pl.BlockSpec((1, d), full),
        ],
        out_specs=[pl.BlockSpec((tm, d), lambda i: (i, 0)), pl.BlockSpec((tm, d), lambda i: (i, 0))],
        out_shape=[jax.ShapeDtypeStruct((n, d), F32), jax.ShapeDtypeStruct((n, d), BF16)],
        compiler_params=_params("arbitrary"),
        name="merge",
    )(ret16, dif16, act, act, x, wa16, wb16, wo16, g2)


def _glu_kernel(*refs, tm, n_t, d_ff, cw, has_state, has_next):
    refs = list(refs)
    h_ref, x1_ref, wu_ref, wd_ref, cwt_ref, cb_ref = refs[:6]
    del refs[:6]
    st_ref = refs.pop(0) if has_state else None
    gn_ref = refs.pop(0) if has_next else None
    y_ref, cs_ref = refs[:2]
    del refs[:2]
    hn_ref = refs.pop(0) if has_next else None
    (carry,) = refs
    ti = pl.program_id(0) % n_t

    @pl.when(ti == 0)
    def _():
        carry[...] = jnp.zeros_like(carry)
        if has_state:
            carry[6:8, :] = st_ref[0]

    h = h_ref[...]
    row8 = lax.broadcasted_iota(jnp.int32, (8, cw), 0)

    def up(c):
        return [_dot(h, wu_ref[:, base:base + cw]) for base in (c * cw, d_ff + c * cw)]

    def rest(c, us, acc):
        conv = []
        for u, base in zip(us, (c * cw, d_ff + c * cw)):
            cols = slice(base, base + cw)
            c6 = jnp.broadcast_to(carry[6:7, cols], (8, cw))
            c7 = jnp.broadcast_to(carry[7:8, cols], (8, cw))
            carry[6:8, cols] = u[tm - 2:tm, :]
            r1 = pltpu.roll(u, 1, 0)
            r2 = pltpu.roll(u, 2, 0)
            top1 = jnp.where(row8 == 0, c7, r1[0:8])
            top2 = jnp.where(row8 == 0, c6, jnp.where(row8 == 1, c7, r2[0:8]))
            if tm > 8:
                sh1 = jnp.concatenate([top1, r1[8:]], axis=0)
                sh2 = jnp.concatenate([top2, r2[8:]], axis=0)
            else:
                sh1, sh2 = top1, top2
            r = cb_ref[:, cols] + cwt_ref[0:1, cols] * sh2
            r = r + cwt_ref[1:2, cols] * sh1
            r = r + cwt_ref[2:3, cols] * u
            conv.append(r)
        a, g = conv
        act = (g * _sigmoid(g) * a).astype(BF16)
        dn = _dot(act, wd_ref[c * cw:(c + 1) * cw, :])
        return dn if acc is None else acc + dn

    n_c = d_ff // cw
    acc = None
    us = up(0)
    for c in range(n_c):
        nxt = up(c + 1) if c + 1 < n_c else None
        acc = rest(c, us, acc)
        us = nxt
    y = x1_ref[...] + acc
    y_ref[...] = y
    if has_next:
        hn_ref[...] = _rms(y, gn_ref[...]).astype(BF16)

    @pl.when(ti == n_t - 1)
    def _():
        cs_ref[0] = carry[6:8, :]


def _glu(h2, x1, wu16, wd16, conv_w, conv_b, state, g_next, *, batch, t_len, tm):
    n, d = x1.shape
    d_ff = wd16.shape[0]
    n_t = t_len // tm
    has_state = state is not None
    has_next = g_next is not None
    full = lambda i: (0, 0)
    row = pl.BlockSpec((tm, d), lambda i: (i, 0))
    in_specs = [row, row, pl.BlockSpec((d, 2 * d_ff), full), pl.BlockSpec((d_ff, d), full),
                pl.BlockSpec((CONV_W, 2 * d_ff), full), pl.BlockSpec((1, 2 * d_ff), full)]
    args = [h2, x1, wu16, wd16, conv_w, conv_b]
    st_spec = pl.BlockSpec((1, CONV_W - 1, 2 * d_ff), lambda i: (i // n_t, 0, 0))
    if has_state:
        in_specs.append(st_spec)
        args.append(state)
    out_specs = [row, st_spec]
    out_shape = [jax.ShapeDtypeStruct((n, d), F32), jax.ShapeDtypeStruct((batch, CONV_W - 1, 2 * d_ff), F32)]
    if has_next:
        in_specs.append(pl.BlockSpec((1, d), full))
        args.append(g_next)
        out_specs.append(row)
        out_shape.append(jax.ShapeDtypeStruct((n, d), BF16))
    res = pl.pallas_call(
        functools.partial(_glu_kernel, tm=tm, n_t=n_t, d_ff=d_ff, cw=MXU_N, has_state=has_state,
                          has_next=has_next),
        grid=(n // tm,),
        in_specs=in_specs,
        out_specs=out_specs,
        out_shape=out_shape,
        scratch_shapes=[pltpu.VMEM((8, 2 * d_ff), F32)],
        compiler_params=_params("arbitrary"),
        name="glu",
    )(*args)
    return res if has_next else (*res, None)


def _rope_tables(pos, reps):
    posf = pos.astype(F32)[:, None]
    lane = np.arange(LANES)
    half = RET_DK // 2
    inv = RET_ROPE_THETA ** (-jnp.arange(half, dtype=F32) / half)
    ang = posf * inv[None, :]
    cos, sin = jnp.cos(ang), jnp.sin(ang)
    rcos = jnp.concatenate([cos, cos], axis=1)
    rsin = jnp.concatenate([-sin, sin], axis=1)
    half = DIFF_ROT // 2
    inv = ROPE_THETA ** (-jnp.arange(half, dtype=F32) / half)
    ang = posf * inv[None, :]
    cos, sin = jnp.cos(ang), jnp.sin(ang)
    grp = lane % DIFF_HD
    sel = jnp.asarray(grp % half)
    cos_l, sin_l = cos[:, sel], sin[:, sel]
    dcos = jnp.where(jnp.asarray(grp < DIFF_ROT)[None, :], cos_l, 1.0)
    dsa = jnp.where(jnp.asarray((grp >= half) & (grp < DIFF_ROT))[None, :], sin_l, 0.0)
    dsb = jnp.where(jnp.asarray(grp < half)[None, :], -sin_l, 0.0)
    return tuple(jnp.tile(t.astype(F32), (reps, 1)) for t in (rcos, rsin, dcos, dsa, dsb))


def _ret_tables(c_len):
    log_g = jnp.log1p(-jnp.exp2(-5.0 - jnp.arange(N_RET_HEADS, dtype=F32)))
    idx = jnp.arange(c_len, dtype=F32)
    rel = idx[:, None] - idx[None, :]
    decay = jnp.where(rel >= 0, jnp.exp(log_g[:, None, None] * jnp.maximum(rel, 0.0)), 0.0)
    xi = jnp.exp(log_g[:, None] * (idx[None, :] + 1.0))
    zeta = jnp.exp(log_g[:, None] * (c_len - 1.0 - idx[None, :]))
    glen = jnp.exp(log_g * c_len)
    return (decay.astype(F32),
            jnp.broadcast_to(xi[:, :, None], (N_RET_HEADS, c_len, RET_DV)).astype(F32),
            jnp.broadcast_to(zeta[:, :, None], (N_RET_HEADS, c_len, RET_DK)).astype(F32),
            jnp.broadcast_to(glen[:, None, None], (N_RET_HEADS, 1, RET_DV)).astype(F32))


def _pick_tile(n, target):
    t = min(n, target)
    while n % t:
        t //= 2
    return t


def kernel(x_prompt, x_sample, cache_k, cache_v, state_ret, state_conv, norm1_g, w_in, ret_norm_g, q_norm_g,
           k_norm_g, lambda_q1, lambda_k1, lambda_q2, lambda_k2, diff_norm_g, w_branch_a, w_branch_b, w_out,
           norm2_g, w_up, conv_w, conv_b, w_down):
    bp, tp, d = x_prompt.shape
    bs, ls, _ = x_sample.shape
    depth = w_in.shape[0]
    past = cache_k.shape[2]
    d_ff = w_down.shape[1]
    hd2 = 2 * DIFF_HD

    tm_p = _pick_tile(tp, 1024)
    tn = d
    c_p = _pick_tile(tp, 256)
    tq_p = _pick_tile(tp, 256)
    tm_glu = _pick_tile(tp, 512)
    ns = bs * ls
    tq_s = LANES
    assert past % LANES == 0 and ls <= tq_s
    blocks_p = tuple((i * tq_p, tq_p, (i + 1) * tq_p) for i in range(tp // tq_p))
    blocks_s = ((0, tq_s, past + tq_s),)

    pos_p = jnp.arange(tp, dtype=jnp.int32)
    pos_s = past + jnp.arange(ls, dtype=jnp.int32)
    tabs_p = _rope_tables(pos_p, 1)
    tabs_s = _rope_tables(pos_s, bs)
    rt_p = _ret_tables(c_p)
    rt_s = _ret_tables(ls)
    blk = np.arange(MXU_N) // DIFF_HD
    gmat = jnp.asarray(blk[:, None] == blk[None, :], dtype=BF16)

    yp = x_prompt.reshape(bp * tp, d)
    ys = x_sample.reshape(ns, d)
    hp = _norm(yp, norm1_g[0].reshape(1, d), tm=tm_p)
    hs = _norm(ys, norm1_g[0].reshape(1, d), tm=ns)
    outs = {k: [] for k in ("kp", "vp", "sp", "cp", "ks", "vs", "ss", "cs")}
    for l in range(depth):
        lam_init = 0.8 - 0.6 * math.exp(-0.3 * l)
        g2 = norm2_g[l].reshape(1, d)
        g_next = norm1_g[l + 1].reshape(1, d) if l + 1 < depth else None
        w16 = w_in[l].astype(BF16)
        wa16 = w_branch_a[l].astype(BF16)
        wb16 = w_branch_b[l].astype(BF16)
        wo16 = w_out[l].astype(BF16)
        wu16 = w_up[l].astype(BF16)
        wd16 = w_down[l].astype(BF16)
        reps = MXU_N // DIFF_HD
        qkg = jnp.stack([jnp.tile(q_norm_g[l], reps), jnp.tile(k_norm_g[l], reps)]).reshape(2, 1, MXU_N)
        rgain = ret_norm_g[l].reshape(1, RET_DV)
        dgain = diff_norm_g[l].reshape(1, hd2)
        lam_p = jnp.stack([lambda_q1[l], lambda_k1[l], lambda_q2[l], lambda_k2[l]])
        cw_l = conv_w[l]
        cb_l = conv_b[l].reshape(1, 2 * d_ff)

        qk_r, act, qk_d, k32, v16, v32 = _proj(hp, w16, tabs_p, qkg, gmat, tm=tm_p, tn=tn)
        ret16, s_fin = _retention(qk_r, act, rt_p, rgain, None, batch=bp, t_len=tp, c_len=c_p)
        dif16 = _attention(qk_d, qk_d, v16, None, None, lam_p, dgain, batch=bp, lq=tp, lk=tp, qcol=0,
                           kcol=N_DIFF_HEADS, vcol=0, blocks=blocks_p, tb=tq_p, causal=True,
                           n_valid_last=0, lam_init=lam_init)
        x1, h2 = _merge(ret16, dif16, act, yp, wa16, wb16, wo16, g2, tm=tm_p)
        yp, cst, hp = _glu(h2, x1, wu16, wd16, cw_l, cb_l, None, g_next, batch=bp, t_len=tp, tm=tm_glu)
        outs["kp"].append(k32.reshape(bp, tp, N_DIFF_HEADS, hd2))
        outs["vp"].append(v32.reshape(bp, tp, N_DIFF_HEADS, hd2))
        outs["sp"].append(s_fin)
        outs["cp"].append(cst)

        qk_r, act, qk_d, k32, v16, v32 = _proj(hs, w16, tabs_s, qkg, gmat, tm=ns, tn=tn)
        ret16, s_fin = _retention(qk_r, act, rt_s, rgain, state_ret[l], batch=bs, t_len=ls, c_len=ls)
        pad_rows = lambda a: jnp.pad(a.reshape(bs, ls, DIFF_W), ((0, 0), (0, tq_s - ls), (0, 0))).reshape(
            bs * tq_s, DIFF_W)
        dif_pad = _attention(pad_rows(qk_d[:, :DIFF_W]), pad_rows(qk_d[:, DIFF_W:]), pad_rows(v16),
                             cache_k, cache_v, lam_p, dgain, batch=bs, lq=tq_s, lk=tq_s, qcol=0, kcol=0,
                             vcol=0, blocks=blocks_s, tb=LANES, causal=False, n_valid_last=ls,
                             lam_init=lam_init, layer=l)
        dif16 = dif_pad.reshape(bs, tq_s, DIFF_W)[:, :ls].reshape(ns, DIFF_W)
        x1, h2 = _merge(ret16, dif16, act, ys, wa16, wb16, wo16, g2, tm=ns)
        ys, cst, hs = _glu(h2, x1, wu16, wd16, cw_l, cb_l, state_conv[l], g_next, batch=bs, t_len=ls, tm=ls)
        outs["ks"].append(k32.reshape(bs, ls, N_DIFF_HEADS, hd2))
        outs["vs"].append(v32.reshape(bs, ls, N_DIFF_HEADS, hd2))
        outs["ss"].append(s_fin)
        outs["cs"].append(cst)

    st = {k: jnp.stack(v) for k, v in outs.items()}
    return (yp.reshape(bp, tp, d), ys.reshape(bs, ls, d), st["kp"], st["vp"], st["sp"], st["cp"],
            st["ks"], st["vs"], st["ss"], st["cs"])
```

```python
import functools
import math

import jax
import jax.numpy as jnp
import numpy as np
from jax import lax
from jax.experimental import pallas as pl
from jax.experimental.pallas import tpu as pltpu

F32 = jnp.float32
BF16 = jnp.bfloat16

EPS = 1e-6
CHUNK = 64
N_RET_HEADS = 4
RET_DK = 128
RET_DV = 256
RET_ROPE_THETA = 10000.0
N_DIFF_HEADS = 8
DIFF_HD = 64
DIFF_ROT = DIFF_HD // 4
ROPE_THETA = 500000.0
CONV_W = 3

RET_QK = N_RET_HEADS * RET_DK
RET_V = N_RET_HEADS * RET_DV
DIFF_W = N_DIFF_HEADS * 2 * DIFF_HD
OFF_QR = 0
OFF_KR = OFF_QR + RET_QK
OFF_VR = OFF_KR + RET_QK
OFF_GR = OFF_VR + RET_V
OFF_QD = OFF_GR + RET_V
OFF_KD = OFF_QD + DIFF_W
OFF_VD = OFF_KD + DIFF_W
OFF_GA = OFF_VD + DIFF_W

LANES = 128
MXU_N = 256
VMEM_LIMIT = 56 * 1024 * 1024
NEG_BIG = -1e30
LOG2E = math.log2(math.e)


def _sigmoid(x):
    return 1.0 / (1.0 + jnp.exp(-x))


def _dot(a, b):
    return jnp.dot(a, b, preferred_element_type=F32)


def _dot_nt(a, b):
    return lax.dot_general(a, b, (((1,), (1,)), ((), ())), preferred_element_type=F32)


def _dot_tn(a, b):
    return lax.dot_general(a, b, (((0,), (0,)), ((), ())), preferred_element_type=F32)


def _params(*sem):
    return pltpu.CompilerParams(dimension_semantics=sem, vmem_limit_bytes=VMEM_LIMIT)


def _rms(x, g):
    ms = jnp.mean(x * x, axis=-1, keepdims=True)
    return x * lax.rsqrt(ms + EPS) * g


def _norm_kernel(x_ref, g_ref, h_ref):
    h_ref[...] = _rms(x_ref[...], g_ref[...]).astype(BF16)


def _norm(x, g, *, tm):
    n, d = x.shape
    return pl.pallas_call(
        _norm_kernel,
        grid=(n // tm,),
        in_specs=[pl.BlockSpec((tm, d), lambda i: (i, 0)), pl.BlockSpec((1, d), lambda i: (0, 0))],
        out_specs=pl.BlockSpec((tm, d), lambda i: (i, 0)),
        out_shape=jax.ShapeDtypeStruct((n, d), BF16),
        compiler_params=_params("arbitrary"),
        name="norm",
    )(x, g)


def _proj_ret_kernel(h_ref, w_ref, rcos_ref, rsin_ref, o_ref):
    y = _dot(h_ref[...], w_ref[...])
    for c in range(y.shape[1] // LANES):
        ys = y[:, c * LANES:(c + 1) * LANES]
        r = ys * rcos_ref[...] + pltpu.roll(ys, RET_DK // 2, 1) * rsin_ref[...]
        if c * LANES >= RET_QK:
            r = r * (RET_DK ** -0.5)
        o_ref[:, c * LANES:(c + 1) * LANES] = r.astype(BF16)


def _proj_act_kernel(h_ref, w_ref, o_ref):
    j = pl.program_id(1)
    y = _dot(h_ref[...], w_ref[...])
    sg = _sigmoid(y)
    o_ref[...] = jnp.where(j == 0, y, jnp.where(j == 1, y * sg, sg)).astype(BF16)


def _proj_diff_kernel(h_ref, w_ref, dcos_ref, dsa_ref, dsb_ref, gain_ref, gmat_ref, o16_ref, o32_ref):
    j = pl.program_id(1)
    scale = jnp.where(j == 0, DIFF_HD ** -0.5 * LOG2E, 1.0).astype(F32)
    h = h_ref[...]
    n_c = w_ref.shape[1] // MXU_N

    def slab(c):
        return _dot(h, w_ref[:, c * MXU_N:(c + 1) * MXU_N])

    nxt = slab(0)
    for c in range(n_c):
        ys, nxt = nxt, (slab(c + 1) if c + 1 < n_c else None)
        ss = _dot((ys * ys).astype(BF16), gmat_ref[...])
        yn = ys * lax.rsqrt(ss * (1.0 / DIFF_HD) + EPS) * gain_ref[0]
        for d in range(MXU_N // LANES):
            z = yn[:, d * LANES:(d + 1) * LANES]
            r = (z * dcos_ref[...] + pltpu.roll(z, DIFF_ROT // 2, 1) * dsa_ref[...]
                 + pltpu.roll(z, LANES - DIFF_ROT // 2, 1) * dsb_ref[...])
            lo = c * MXU_N + d * LANES
            o32_ref[:, lo:lo + LANES] = r
            o16_ref[:, lo:lo + LANES] = (r * scale).astype(BF16)


def _proj_val_kernel(h_ref, w_ref, o16_ref, o32_ref):
    y = _dot(h_ref[...], w_ref[...])
    o32_ref[...] = y
    o16_ref[...] = y.astype(BF16)


def _proj(h16, w16, tabs, qkg, gmat, *, tm, tn):
    n, d = h16.shape
    rcos, rsin, dcos, dsa, dsb = tabs
    nt = rcos.shape[0] // tm
    h_spec = pl.BlockSpec((tm, d), lambda i, j: (i, 0))
    tab_spec = pl.BlockSpec((tm, LANES), lambda i, j: (i % nt, 0))
    o_spec = pl.BlockSpec((tm, tn), lambda i, j: (i, j))
    sem = ("arbitrary", "arbitrary")
    assert tn == 2 * RET_QK == RET_V == DIFF_W == d

    def w_spec(col0):
        return pl.BlockSpec((d, tn), lambda i, j: (0, col0 // tn + j))

    qk_r = pl.pallas_call(
        _proj_ret_kernel, grid=(n // tm, 1),
        in_specs=[h_spec, w_spec(OFF_QR), tab_spec, tab_spec],
        out_specs=o_spec, out_shape=jax.ShapeDtypeStruct((n, tn), BF16),
        compiler_params=_params(*sem), name="proj_ret",
    )(h16, w16, rcos, rsin)

    act_w = pl.BlockSpec((d, tn), lambda i, j: (0, jnp.where(j < 2, OFF_VR // tn + j, OFF_GA // tn + j - 2)))
    act = pl.pallas_call(
        _proj_act_kernel, grid=(n // tm, 4),
        in_specs=[h_spec, act_w],
        out_specs=o_spec, out_shape=jax.ShapeDtypeStruct((n, 4 * tn), BF16),
        compiler_params=_params(*sem), name="proj_act",
    )(h16, w16)

    qk_d, k32 = pl.pallas_call(
        _proj_diff_kernel, grid=(n // tm, 2),
        in_specs=[h_spec, w_spec(OFF_QD), tab_spec, tab_spec, tab_spec,
                  pl.BlockSpec((1, 1, MXU_N), lambda i, j: (j, 0, 0)),
                  pl.BlockSpec((MXU_N, MXU_N), lambda i, j: (0, 0))],
        out_specs=[o_spec, pl.BlockSpec((tm, tn), lambda i, j: (i, 0))],
        out_shape=[jax.ShapeDtypeStruct((n, 2 * tn), BF16), jax.ShapeDtypeStruct((n, tn), F32)],
        compiler_params=_params(*sem), name="proj_diff",
    )(h16, w16, dcos, dsa, dsb, qkg, gmat)

    v16, v32 = pl.pallas_call(
        _proj_val_kernel, grid=(n // tm, 1),
        in_specs=[h_spec, w_spec(OFF_VD)],
        out_specs=[o_spec, o_spec],
        out_shape=[jax.ShapeDtypeStruct((n, tn), BF16), jax.ShapeDtypeStruct((n, tn), F32)],
        compiler_params=_params(*sem), name="proj_val",
    )(h16, w16)
    return qk_r, act, qk_d, k32, v16, v32


def _ret_kernel(*refs, c_len, n_chunks, has_state):
    if has_state:
        (q_ref, k_ref, v_ref, g_ref, decay_ref, xi_ref, zeta_ref, glen_ref, gain_ref, s0_ref,
         o_ref, s_ref) = refs
        s = s0_ref[0, 0]
    else:
        (q_ref, k_ref, v_ref, g_ref, decay_ref, xi_ref, zeta_ref, glen_ref, gain_ref,
         o_ref, s_ref) = refs
        s = jnp.zeros((RET_DK, RET_DV), F32)
    decay = decay_ref[0]
    for c in range(n_chunks):
        rows = slice(c * c_len, (c + 1) * c_len)
        q = q_ref[rows, :]
        k = k_ref[rows, :]
        v = v_ref[rows, :]
        sc = _dot_nt(q, k) * decay
        o = _dot(sc.astype(BF16), v) + _dot(q, s.astype(BF16)) * xi_ref[0]
        kz = (k.astype(F32) * zeta_ref[0]).astype(BF16)
        s = s * glen_ref[0] + _dot_tn(kz, v)
        ms = jnp.mean(o * o, axis=-1, keepdims=True)
        on = o * lax.rsqrt(ms + EPS) * gain_ref[...]
        o_ref[rows, :] = (on * g_ref[rows, :].astype(F32)).astype(BF16)
    s_ref[0, 0] = s


def _retention(qk_r, act, tables, gain, s0, *, batch, t_len, c_len):
    n = qk_r.shape[0]
    decay, xi, zeta, glen = tables
    has_state = s0 is not None
    in_specs = [
        pl.BlockSpec((t_len, RET_DK), lambda b, h: (b, h)),
        pl.BlockSpec((t_len, RET_DK), lambda b, h: (b, N_RET_HEADS + h)),
        pl.BlockSpec((t_len, RET_DV), lambda b, h: (b, h)),
        pl.BlockSpec((t_len, RET_DV), lambda b, h: (b, N_RET_HEADS + h)),
        pl.BlockSpec((1, c_len, c_len), lambda b, h: (h, 0, 0)),
        pl.BlockSpec((1, c_len, RET_DV), lambda b, h: (h, 0, 0)),
        pl.BlockSpec((1, c_len, RET_DK), lambda b, h: (h, 0, 0)),
        pl.BlockSpec((1, 1, RET_DV), lambda b, h: (h, 0, 0)),
        pl.BlockSpec((1, RET_DV), lambda b, h: (0, 0)),
    ]
    args = [qk_r, qk_r, act, act, decay, xi, zeta, glen, gain]
    if has_state:
        in_specs.append(pl.BlockSpec((1, 1, RET_DK, RET_DV), lambda b, h: (b, h, 0, 0)))
        args.append(s0)
    return pl.pallas_call(
        functools.partial(_ret_kernel, c_len=c_len, n_chunks=t_len // c_len, has_state=has_state),
        grid=(batch, N_RET_HEADS),
        in_specs=in_specs,
        out_specs=[
            pl.BlockSpec((t_len, RET_DV), lambda b, h: (b, h)),
            pl.BlockSpec((1, 1, RET_DK, RET_DV), lambda b, h: (b, h, 0, 0)),
        ],
        out_shape=[
            jax.ShapeDtypeStruct((n, RET_V), BF16),
            jax.ShapeDtypeStruct((batch, N_RET_HEADS, RET_DK, RET_DV), F32),
        ],
        compiler_params=_params("arbitrary", "arbitrary"),
        name="ret",
    )(*args)


def _attn_kernel(q_ref, k_ref, v_ref, kc_ref, vc_ref, lam_ref, gain_ref, o_ref, vt_scr, s_scr, p_scr, *, blocks,
                 tb, causal, n_valid_last, lam_init, head=None):
    cs = slice(None)
    n_cache = 0 if kc_ref is None else kc_ref.shape[1]
    lk = n_cache + k_ref.shape[0]
    tq = blocks[0][1]

    def k_tile(t):
        r0 = t * tb
        if r0 < n_cache:
            return kc_ref[head, r0:r0 + tb, :]
        return k_ref[r0 - n_cache:r0 - n_cache + tb, cs]

    if n_cache:
        vt_scr[:, 0:n_cache] = vc_ref[head]
    for t in range(n_cache // tb, lk // tb):
        r0 = t * tb
        v = v_ref[r0 - n_cache:r0 - n_cache + tb, cs].astype(F32)
        vt_scr[:, r0:r0 + tb] = v.T.astype(BF16)
    lp = lam_ref[...]
    lam = (jnp.exp(jnp.sum(lp[0:1] * lp[1:2], axis=1, keepdims=True))
           - jnp.exp(jnp.sum(lp[2:3] * lp[3:4], axis=1, keepdims=True)) + lam_init)
    kidx = lax.broadcasted_iota(jnp.int32, (tb, tq), 0)
    if causal:
        qidx = lax.broadcasted_iota(jnp.int32, (tb, tq), 1)
        visible = (kidx // CHUNK) <= (qidx // CHUNK)
    else:
        visible = kidx < n_valid_last
    bias = jnp.where(visible, 0.0, NEG_BIG).astype(F32)

    def score_tiles(bi):
        q0, _, nk = blocks[bi]
        slot = bi % 2
        q = q_ref[q0:q0 + tq, cs]
        lane = lax.broadcasted_iota(jnp.int32, q.shape, 1)
        qs = (jnp.where(lane < DIFF_HD, q, jnp.zeros_like(q)), jnp.where(lane >= DIFF_HD, q, jnp.zeros_like(q)))
        nt = nk // tb
        st = {"m": [None, None]}

        def tile(t):
            kt = k_tile(t)
            for c in range(2):
                s = _dot_nt(kt, qs[c])
                if t == nt - 1:
                    s = s + bias
                s_scr[slot, c, t] = s
                mt = jnp.max(s, axis=0, keepdims=True)
                st["m"][c] = mt if st["m"][c] is None else jnp.maximum(st["m"][c], mt)

        return [functools.partial(tile, t) for t in range(nt)], st

    def prob_tiles(bi, st1):
        q0, _, nk = blocks[bi]
        slot = bi % 2
        st = {"l": [None, None]}

        def tile(t):
            for c in range(2):
                p = jnp.exp2(s_scr[slot, c, t] - st1["m"][c])
                lt = jnp.sum(p, axis=0, keepdims=True)
                p_scr[slot, c, t * tb:(t + 1) * tb, :] = p.astype(BF16)
                st["l"][c] = lt if st["l"][c] is None else st["l"][c] + lt

        def finish():
            a0, a1 = (_dot(vt_scr[:, 0:nk], p_scr[slot, c, 0:nk, :]) for c in range(2))
            o = a0 * (1.0 / st["l"][0]) - (lam * (1.0 / st["l"][1])) * a1
            ms = jnp.mean(o * o, axis=0, keepdims=True)
            on = (o * lax.rsqrt(ms + EPS)).T
            o_ref[q0:q0 + tq, cs] = (on * gain_ref[...] * (1.0 - lam_init)).astype(BF16)

        return [functools.partial(tile, t) for t in range(nk // tb)], finish

    nb = len(blocks)
    t1, st1 = score_tiles(0)
    for f in t1:
        f()
    for bi in range(nb):
        t2, finish = prob_tiles(bi, st1)
        t1n, st1n = score_tiles(bi + 1) if bi + 1 < nb else ([], None)
        for t in range(max(len(t2), len(t1n))):
            if t < len(t1n):
                t1n[t]()
            if t < len(t2):
                t2[t]()
        finish()
        st1 = st1n


def _attention(q_arr, k_arr, v_arr, kc_arr, vc_arr, lam_p, gain, *, batch, lq, lk, qcol, kcol, vcol, blocks, tb,
               causal, n_valid_last, lam_init, layer=0):
    hd2 = 2 * DIFF_HD
    tq = blocks[0][1]
    has_cache = kc_arr is not None
    past = kc_arr.shape[2] if has_cache else 0
    lk_all = past + lk
    scratch = [pltpu.VMEM((hd2, lk_all), BF16),
               pltpu.VMEM((2, 2, lk_all // tb, tb, tq), F32),
               pltpu.VMEM((2, 2, lk_all, tq), BF16)]
    static = dict(blocks=blocks, tb=tb, causal=causal, n_valid_last=n_valid_last, lam_init=lam_init)
    small = lambda shape: pl.BlockSpec(shape, lambda *_: (0, 0))
    out_shape = jax.ShapeDtypeStruct((batch * lq, DIFF_W), BF16)

    if not has_cache:
        def body(q_ref, k_ref, v_ref, lam_ref, gain_ref, o_ref, vt_scr, s_scr, p_scr):
            _attn_kernel(q_ref, k_ref, v_ref, None, None, lam_ref, gain_ref, o_ref, vt_scr, s_scr, p_scr, **static)

        row = lambda col: (lambda b, h: (b, col + h))
        return pl.pallas_call(
            body,
            grid=(batch, N_DIFF_HEADS),
            in_specs=[pl.BlockSpec((lq, hd2), row(qcol)), pl.BlockSpec((lk, hd2), row(kcol)),
                      pl.BlockSpec((lk, hd2), row(vcol)), small((4, DIFF_HD)), small((1, hd2))],
            out_specs=pl.BlockSpec((lq, hd2), lambda b, h: (b, h)),
            out_shape=out_shape,
            scratch_shapes=scratch,
            compiler_params=_params("arbitrary", "arbitrary"),
            name="attn",
        )(q_arr, k_arr, v_arr, lam_p, gain)

    def body_cache(q_ref, k_ref, v_ref, kc_ref, vc_ref, lam_ref, gain_ref, o_ref, vt_scr, s_scr, p_scr, kh_scr,
                   vth_scr):
        h = pl.program_id(1)

        @pl.when(h == 0)
        def _():
            for r0 in range(0, past, tb):
                kh_scr[:, r0:r0 + tb, :] = pltpu.einshape("khd->hkd", kc_ref[r0:r0 + tb]).astype(BF16)
                vh = pltpu.einshape("khd->hkd", vc_ref[r0:r0 + tb])
                for hh in range(N_DIFF_HEADS):
                    vth_scr[hh, :, r0:r0 + tb] = vh[hh].T.astype(BF16)

        _attn_kernel(q_ref, k_ref, v_ref, kh_scr, vth_scr, lam_ref, gain_ref, o_ref, vt_scr, s_scr, p_scr,
                     head=h, **static)

    scratch += [pltpu.VMEM((N_DIFF_HEADS, past, hd2), BF16), pltpu.VMEM((N_DIFF_HEADS, hd2, past), BF16)]
    row = lambda col: (lambda b, h: (b, col + h))
    cache_spec = pl.BlockSpec((None, None, past, N_DIFF_HEADS, hd2), lambda b, h: (layer, b, 0, 0, 0))
    return pl.pallas_call(
        body_cache,
        grid=(batch, N_DIFF_HEADS),
        in_specs=[pl.BlockSpec((lq, hd2), row(qcol)), pl.BlockSpec((lk, hd2), row(kcol)),
                  pl.BlockSpec((lk, hd2), row(vcol)), cache_spec, cache_spec, small((4, DIFF_HD)),
                  small((1, hd2))],
        out_specs=pl.BlockSpec((lq, hd2), lambda b, h: (b, h)),
        out_shape=out_shape,
        scratch_shapes=scratch,
        compiler_params=_params("arbitrary", "arbitrary"),
        name="attn_cache",
    )(q_arr, k_arr, v_arr, kc_arr, vc_arr, lam_p, gain)


def _merge_kernel(ret_ref, dif_ref, sa_ref, sb_ref, x_ref, wa_ref, wb_ref, wo_ref, g2_ref, x1_ref, h2_ref):
    ya = _dot(ret_ref[...], wa_ref[...])
    yb = _dot(dif_ref[...], wb_ref[...])
    y = sa_ref[...].astype(F32) * ya + sb_ref[...].astype(F32) * yb
    x1 = x_ref[...] + _dot(y.astype(BF16), wo_ref[...])
    x1_ref[...] = x1
    h2_ref[...] = _rms(x1, g2_ref[...]).astype(BF16)


def _merge(ret16, dif16, act, x, wa16, wb16, wo16, g2, *, tm):
    n, d = x.shape
    full = lambda i: (0, 0)
    return pl.pallas_call(
        _merge_kernel,
        grid=(n // tm,),
        in_specs=[
            pl.BlockSpec((tm, RET_V), lambda i: (i, 0)),
            pl.BlockSpec((tm, DIFF_W), lambda i: (i, 0)),
            pl.BlockSpec((tm, d), lambda i: (i, 2)),
            pl.BlockSpec((tm, d), lambda i: (i, 3)),
            pl.BlockSpec((tm, d), lambda i: (i, 0)),
            pl.BlockSpec((RET_V, d), full),
            pl.BlockSpec((DIFF_W, d), full),
            pl.BlockSpec((d, d), full),
            pl.BlockSpec((1, d), full),
        ],
        out_specs=[pl.BlockSpec((tm, d), lambda i: (i, 0)), pl.BlockSpec((tm, d), lambda i: (i, 0))],
        out_shape=[jax.ShapeDtypeStruct((n, d), F32), jax.ShapeDtypeStruct((n, d), BF16)],
        compiler_params=_params("arbitrary"),
        name="merge",
    )(ret16, dif16, act, act, x, wa16, wb16, wo16, g2)


def _glu_kernel(*refs, tm, n_t, d_ff, cw, has_state, has_next):
    refs = list(refs)
    h_ref, x1_ref, wu_ref, wd_ref, cwt_ref, cb_ref = refs[:6]
    del refs[:6]
    st_ref = refs.pop(0) if has_state else None
    gn_ref = refs.pop(0) if has_next else None
    y_ref, cs_ref = refs[:2]
    del refs[:2]
    hn_ref = refs.pop(0) if has_next else None
    (carry,) = refs
    ti = pl.program_id(0) % n_t

    @pl.when(ti == 0)
    def _():
        carry[...] = jnp.zeros_like(carry)
        if has_state:
            carry[6:8, :] = st_ref[0]

    h = h_ref[...]
    row8 = lax.broadcasted_iota(jnp.int32, (8, cw), 0)

    def up(c):
        return [_dot(h, wu_ref[:, base:base + cw]) for base in (c * cw, d_ff + c * cw)]

    def rest(c, us, acc):
        conv = []
        for u, base in zip(us, (c * cw, d_ff + c * cw)):
            cols = slice(base, base + cw)
            c6 = jnp.broadcast_to(carry[6:7, cols], (8, cw))
            c7 = jnp.broadcast_to(carry[7:8, cols], (8, cw))
            carry[6:8, cols] = u[tm - 2:tm, :]
            r1 = pltpu.roll(u, 1, 0)
            r2 = pltpu.roll(u, 2, 0)
            top1 = jnp.where(row8 == 0, c7, r1[0:8])
            top2 = jnp.where(row8 == 0, c6, jnp.where(row8 == 1, c7, r2[0:8]))
            if tm > 8:
                sh1 = jnp.concatenate([top1, r1[8:]], axis=0)
                sh2 = jnp.concatenate([top2, r2[8:]], axis=0)
            else:
                sh1, sh2 = top1, top2
            r = cb_ref[:, cols] + cwt_ref[0:1, cols] * sh2
            r = r + cwt_ref[1:2, cols] * sh1
            r = r + cwt_ref[2:3, cols] * u
            conv.append(r)
        a, g = conv
        act = (g * _sigmoid(g) * a).astype(BF16)
        dn = _dot(act, wd_ref[c * cw:(c + 1) * cw, :])
        return dn if acc is None else acc + dn

    n_c = d_ff // cw
    acc = None
    us = up(0)
    for c in range(n_c):
        nxt = up(c + 1) if c + 1 < n_c else None
        acc = rest(c, us, acc)
        us = nxt
    y = x1_ref[...] + acc
    y_ref[...] = y
    if has_next:
        hn_ref[...] = _rms(y, gn_ref[...]).astype(BF16)

    @pl.when(ti == n_t - 1)
    def _():
        cs_ref[0] = carry[6:8, :]


def _glu(h2, x1, wu16, wd16, conv_w, conv_b, state, g_next, *, batch, t_len, tm):
    n, d = x1.shape
    d_ff = wd16.shape[0]
    n_t = t_len // tm
    has_state = state is not None
    has_next = g_next is not None
    full = lambda i: (0, 0)
    row = pl.BlockSpec((tm, d), lambda i: (i, 0))
    in_specs = [row, row, pl.BlockSpec((d, 2 * d_ff), full), pl.BlockSpec((d_ff, d), full),
                pl.BlockSpec((CONV_W, 2 * d_ff), full), pl.BlockSpec((1, 2 * d_ff), full)]
    args = [h2, x1, wu16, wd16, conv_w, conv_b]
    st_spec = pl.BlockSpec((1, CONV_W - 1, 2 * d_ff), lambda i: (i // n_t, 0, 0))
    if has_state:
        in_specs.append(st_spec)
        args.append(state)
    out_specs = [row, st_spec]
    out_shape = [jax.ShapeDtypeStruct((n, d), F32), jax.ShapeDtypeStruct((batch, CONV_W - 1, 2 * d_ff), F32)]
    if has_next:
        in_specs.append(pl.BlockSpec((1, d), full))
        args.append(g_next)
        out_specs.append(row)
        out_shape.append(jax.ShapeDtypeStruct((n, d), BF16))
    res = pl.pallas_call(
        functools.partial(_glu_kernel, tm=tm, n_t=n_t, d_ff=d_ff, cw=MXU_N, has_state=has_state,
                          has_next=has_next),
        grid=(n // tm,),
        in_specs=in_specs,
        out_specs=out_specs,
        out_shape=out_shape,
        scratch_shapes=[pltpu.VMEM((8, 2 * d_ff), F32)],
        compiler_params=_params("arbitrary"),
        name="glu",
    )(*args)
    return res if has_next else (*res, None)


def _rope_tables(pos, reps):
    posf = pos.astype(F32)[:, None]
    lane = np.arange(LANES)
    half = RET_DK // 2
    inv = RET_ROPE_THETA ** (-jnp.arange(half, dtype=F32) / half)
    ang = posf * inv[None, :]
    cos, sin = jnp.cos(ang), jnp.sin(ang)
    rcos = jnp.concatenate([cos, cos], axis=1)
    rsin = jnp.concatenate([-sin, sin], axis=1)
    half = DIFF_ROT // 2
    inv = ROPE_THETA ** (-jnp.arange(half, dtype=F32) / half)
    ang = posf * inv[None, :]
    cos, sin = jnp.cos(ang), jnp.sin(ang)
    grp = lane % DIFF_HD
    sel = jnp.asarray(grp % half)
    cos_l, sin_l = cos[:, sel], sin[:, sel]
    dcos = jnp.where(jnp.asarray(grp < DIFF_ROT)[None, :], cos_l, 1.0)
    dsa = jnp.where(jnp.asarray((grp >= half) & (grp < DIFF_ROT))[None, :], sin_l, 0.0)
    dsb = jnp.where(jnp.asarray(grp < half)[None, :], -sin_l, 0.0)
    return tuple(jnp.tile(t.astype(F32), (reps, 1)) for t in (rcos, rsin, dcos, dsa, dsb))


def _ret_tables(c_len):
    log_g = jnp.log1p(-jnp.exp2(-5.0 - jnp.arange(N_RET_HEADS, dtype=F32)))
    idx = jnp.arange(c_len, dtype=F32)
    rel = idx[:, None] - idx[None, :]
    decay = jnp.where(rel >= 0, jnp.exp(log_g[:, None, None] * jnp.maximum(rel, 0.0)), 0.0)
    xi = jnp.exp(log_g[:, None] * (idx[None, :] + 1.0))
    zeta = jnp.exp(log_g[:, None] * (c_len - 1.0 - idx[None, :]))
    glen = jnp.exp(log_g * c_len)
    return (decay.astype(F32),
            jnp.broadcast_to(xi[:, :, None], (N_RET_HEADS, c_len, RET_DV)).astype(F32),
            jnp.broadcast_to(zeta[:, :, None], (N_RET_HEADS, c_len, RET_DK)).astype(F32),
            jnp.broadcast_to(glen[:, None, None], (N_RET_HEADS, 1, RET_DV)).astype(F32))


def _pick_tile(n, target):
    t = min(n, target)
    while n % t:
        t //= 2
    return t


def kernel(x_prompt, x_sample, cache_k, cache_v, state_ret, state_conv, norm1_g, w_in, ret_norm_g, q_norm_g,
           k_norm_g, lambda_q1, lambda_k1, lambda_q2, lambda_k2, diff_norm_g, w_branch_a, w_branch_b, w_out,
           norm2_g, w_up, conv_w, conv_b, w_down):
    bp, tp, d = x_prompt.shape
    bs, ls, _ = x_sample.shape
    depth = w_in.shape[0]
    past = cache_k.shape[2]
    d_ff = w_down.shape[1]
    hd2 = 2 * DIFF_HD

    tm_p = _pick_tile(tp, 1024)
    tn = d
    c_p = _pick_tile(tp, 256)
    tq_p = _pick_tile(tp, 256)
    tm_glu = _pick_tile(tp, 512)
    ns = bs * ls
    tq_s = LANES
    assert past % LANES == 0 and ls <= tq_s
    blocks_p = tuple((i * tq_p, tq_p, (i + 1) * tq_p) for i in range(tp // tq_p))
    blocks_s = ((0, tq_s, past + tq_s),)

    pos_p = jnp.arange(tp, dtype=jnp.int32)
    pos_s = past + jnp.arange(ls, dtype=jnp.int32)
    tabs_p = _rope_tables(pos_p, 1)
    tabs_s = _rope_tables(pos_s, bs)
    rt_p = _ret_tables(c_p)
    rt_s = _ret_tables(ls)
    blk = np.arange(MXU_N) // DIFF_HD
    gmat = jnp.asarray(blk[:, None] == blk[None, :], dtype=BF16)

    yp = x_prompt.reshape(bp * tp, d)
    ys = x_sample.reshape(ns, d)
    hp = _norm(yp, norm1_g[0].reshape(1, d), tm=tm_p)
    hs = _norm(ys, norm1_g[0].reshape(1, d), tm=ns)
    outs = {k: [] for k in ("kp", "vp", "sp", "cp", "ks", "vs", "ss", "cs")}
    for l in range(depth):
        lam_init = 0.8 - 0.6 * math.exp(-0.3 * l)
        g2 = norm2_g[l].reshape(1, d)
        g_next = norm1_g[l + 1].reshape(1, d) if l + 1 < depth else None
        w16 = w_in[l].astype(BF16)
        wa16 = w_branch_a[l].astype(BF16)
        wb16 = w_branch_b[l].astype(BF16)
        wo16 = w_out[l].astype(BF16)
        wu16 = w_up[l].astype(BF16)
        wd16 = w_down[l].astype(BF16)
        reps = MXU_N // DIFF_HD
        qkg = jnp.stack([jnp.tile(q_norm_g[l], reps), jnp.tile(k_norm_g[l], reps)]).reshape(2, 1, MXU_N)
        rgain = ret_norm_g[l].reshape(1, RET_DV)
        dgain = diff_norm_g[l].reshape(1, hd2)
        lam_p = jnp.stack([lambda_q1[l], lambda_k1[l], lambda_q2[l], lambda_k2[l]])
        cw_l = conv_w[l]
        cb_l = conv_b[l].reshape(1, 2 * d_ff)

        qk_r, act, qk_d, k32, v16, v32 = _proj(hp, w16, tabs_p, qkg, gmat, tm=tm_p, tn=tn)
        ret16, s_fin = _retention(qk_r, act, rt_p, rgain, None, batch=bp, t_len=tp, c_len=c_p)
        dif16 = _attention(qk_d, qk_d, v16, None, None, lam_p, dgain, batch=bp, lq=tp, lk=tp, qcol=0,
                           kcol=N_DIFF_HEADS, vcol=0, blocks=blocks_p, tb=tq_p, causal=True,
                           n_valid_last=0, lam_init=lam_init)
        x1, h2 = _merge(ret16, dif16, act, yp, wa16, wb16, wo16, g2, tm=tm_p)
        yp, cst, hp = _glu(h2, x1, wu16, wd16, cw_l, cb_l, None, g_next, batch=bp, t_len=tp, tm=tm_glu)
        outs["kp"].append(k32.reshape(bp, tp, N_DIFF_HEADS, hd2))
        outs["vp"].append(v32.reshape(bp, tp, N_DIFF_HEADS, hd2))
        outs["sp"].append(s_fin)
        outs["cp"].append(cst)

        qk_r, act, qk_d, k32, v16, v32 = _proj(hs, w16, tabs_s, qkg, gmat, tm=ns, tn=tn)
        ret16, s_fin = _retention(qk_r, act, rt_s, rgain, state_ret[l], batch=bs, t_len=ls, c_len=ls)
        pad_rows = lambda a: jnp.pad(a.reshape(bs, ls, DIFF_W), ((0, 0), (0, tq_s - ls), (0, 0))).reshape(
            bs * tq_s, DIFF_W)
        dif_pad = _attention(pad_rows(qk_d[:, :DIFF_W]), pad_rows(qk_d[:, DIFF_W:]), pad_rows(v16),
                             cache_k, cache_v, lam_p, dgain, batch=bs, lq=tq_s, lk=tq_s, qcol=0, kcol=0,
                             vcol=0, blocks=blocks_s, tb=LANES, causal=False, n_valid_last=ls,
                             lam_init=lam_init, layer=l)
        dif16 = dif_pad.reshape(bs, tq_s, DIFF_W)[:, :ls].reshape(ns, DIFF_W)
        x1, h2 = _merge(ret16, dif16, act, ys, wa16, wb16, wo16, g2, tm=ns)
        ys, cst, hs = _glu(h2, x1, wu16, wd16, cw_l, cb_l, state_conv[l], g_next, batch=bs, t_len=ls, tm=ls)
        outs["ks"].append(k32.reshape(bs, ls, N_DIFF_HEADS, hd2))
        outs["vs"].append(v32.reshape(bs, ls, N_DIFF_HEADS, hd2))
        outs["ss"].append(s_fin)
        outs["cs"].append(cst)

    st = {k: jnp.stack(v) for k, v in outs.items()}
    return (yp.reshape(bp, tp, d), ys.reshape(bs, ls, d), st["kp"], st["vp"], st["sp"], st["cp"],
            st["ks"], st["vs"], st["ss"], st["cs"])
```

```python
import functools
import math

import jax
import jax.numpy as jnp
import numpy as np
from jax import lax
from jax.experimental import pallas as pl
from jax.experimental.pallas import tpu as pltpu

F32 = jnp.float32
BF16 = jnp.bfloat16

EPS = 1e-6
CHUNK = 64
N_RET_HEADS = 4
RET_DK = 128
RET_DV = 256
RET_ROPE_THETA = 10000.0
N_DIFF_HEADS = 8
DIFF_HD = 64
DIFF_ROT = DIFF_HD // 4
ROPE_THETA = 500000.0
CONV_W = 3

RET_QK = N_RET_HEADS * RET_DK
RET_V = N_RET_HEADS * RET_DV
DIFF_W = N_DIFF_HEADS * 2 * DIFF_HD
OFF_QR = 0
OFF_KR = OFF_QR + RET_QK
OFF_VR = OFF_KR + RET_QK
OFF_GR = OFF_VR + RET_V
OFF_QD = OFF_GR + RET_V
OFF_KD = OFF_QD + DIFF_W
OFF_VD = OFF_KD + DIFF_W
OFF_GA = OFF_VD + DIFF_W

LANES = 128
MXU_N = 256
VMEM_LIMIT = 56 * 1024 * 1024
NEG_BIG = -1e30
LOG2E = math.log2(math.e)


def _sigmoid(x):
    return 1.0 / (1.0 + jnp.exp(-x))


def _dot(a, b):
    return jnp.dot(a, b, preferred_element_type=F32)


def _dot_nt(a, b):
    return lax.dot_general(a, b, (((1,), (1,)), ((), ())), preferred_element_type=F32)


def _dot_tn(a, b):
    return lax.dot_general(a, b, (((0,), (0,)), ((), ())), preferred_element_type=F32)


def _params(*sem):
    return pltpu.CompilerParams(dimension_semantics=sem, vmem_limit_bytes=VMEM_LIMIT)


def _rms(x, g):
    ms = jnp.mean(x * x, axis=-1, keepdims=True)
    return x * lax.rsqrt(ms + EPS) * g


def _norm_kernel(x_ref, g_ref, h_ref):
    h_ref[...] = _rms(x_ref[...], g_ref[...]).astype(BF16)


def _norm(x, g, *, tm):
    n, d = x.shape
    return pl.pallas_call(
        _norm_kernel,
        grid=(n // tm,),
        in_specs=[pl.BlockSpec((tm, d), lambda i: (i, 0)), pl.BlockSpec((1, d), lambda i: (0, 0))],
        out_specs=pl.BlockSpec((tm, d), lambda i: (i, 0)),
        out_shape=jax.ShapeDtypeStruct((n, d), BF16),
        compiler_params=_params("arbitrary"),
        name="norm",
    )(x, g)


def _proj_ret_kernel(h_ref, w_ref, rcos_ref, rsin_ref, o_ref):
    y = _dot(h_ref[...], w_ref[...])
    for c in range(y.shape[1] // LANES):
        ys = y[:, c * LANES:(c + 1) * LANES]
        r = ys * rcos_ref[...] + pltpu.roll(ys, RET_DK // 2, 1) * rsin_ref[...]
        if c * LANES >= RET_QK:
            r = r * (RET_DK ** -0.5)
        o_ref[:, c * LANES:(c + 1) * LANES] = r.astype(BF16)


def _proj_act_kernel(h_ref, w_ref, o_ref):
    j = pl.program_id(1)
    y = _dot(h_ref[...], w_ref[...])
    sg = _sigmoid(y)
    o_ref[...] = jnp.where(j == 0, y, jnp.where(j == 1, y * sg, sg)).astype(BF16)


def _proj_diff_kernel(h_ref, w_ref, dcos_ref, dsa_ref, dsb_ref, gain_ref, gmat_ref, o16_ref, o32_ref):
    j = pl.program_id(1)
    scale = jnp.where(j == 0, DIFF_HD ** -0.5 * LOG2E, 1.0).astype(F32)
    h = h_ref[...]
    n_c = w_ref.shape[1] // MXU_N

    def slab(c):
        return _dot(h, w_ref[:, c * MXU_N:(c + 1) * MXU_N])

    nxt = slab(0)
    for c in range(n_c):
        ys, nxt = nxt, (slab(c + 1) if c + 1 < n_c else None)
        ss = _dot((ys * ys).astype(BF16), gmat_ref[...])
        yn = ys * lax.rsqrt(ss * (1.0 / DIFF_HD) + EPS) * gain_ref[0]
        for d in range(MXU_N // LANES):
            z = yn[:, d * LANES:(d + 1) * LANES]
            r = (z * dcos_ref[...] + pltpu.roll(z, DIFF_ROT // 2, 1) * dsa_ref[...]
                 + pltpu.roll(z, LANES - DIFF_ROT // 2, 1) * dsb_ref[...])
            lo = c * MXU_N + d * LANES
            o32_ref[:, lo:lo + LANES] = r
            o16_ref[:, lo:lo + LANES] = (r * scale).astype(BF16)


def _proj_val_kernel(h_ref, w_ref, o32_ref):
    o32_ref[...] = _dot(h_ref[...], w_ref[...])


def _proj(h16, w16, tabs, qkg, gmat, *, tm, tn):
    n, d = h16.shape
    rcos, rsin, dcos, dsa, dsb = tabs
    nt = rcos.shape[0] // tm
    h_spec = pl.BlockSpec((tm, d), lambda i, j: (i, 0))
    tab_spec = pl.BlockSpec((tm, LANES), lambda i, j: (i % nt, 0))
    o_spec = pl.BlockSpec((tm, tn), lambda i, j: (i, j))
    sem = ("arbitrary", "arbitrary")
    assert tn == 2 * RET_QK == RET_V == DIFF_W == d

    def w_spec(col0):
        return pl.BlockSpec((d, tn), lambda i, j: (0, col0 // tn + j))

    qk_r = pl.pallas_call(
        _proj_ret_kernel, grid=(n // tm, 1),
        in_specs=[h_spec, w_spec(OFF_QR), tab_spec, tab_spec],
        out_specs=o_spec, out_shape=jax.ShapeDtypeStruct((n, tn), BF16),
        compiler_params=_params(*sem), name="proj_ret",
    )(h16, w16, rcos, rsin)

    act_w = pl.BlockSpec((d, tn), lambda i, j: (0, jnp.where(j < 2, OFF_VR // tn + j, OFF_GA // tn + j - 2)))
    act = pl.pallas_call(
        _proj_act_kernel, grid=(n // tm, 4),
        in_specs=[h_spec, act_w],
        out_specs=o_spec, out_shape=jax.ShapeDtypeStruct((n, 4 * tn), BF16),
        compiler_params=_params(*sem), name="proj_act",
    )(h16, w16)

    qk_d, k32 = pl.pallas_call(
        _proj_diff_kernel, grid=(n // tm, 2),
        in_specs=[h_spec, w_spec(OFF_QD), tab_spec, tab_spec, tab_spec,
                  pl.BlockSpec((1, 1, MXU_N), lambda i, j: (j, 0, 0)),
                  pl.BlockSpec((MXU_N, MXU_N), lambda i, j: (0, 0))],
        out_specs=[o_spec, pl.BlockSpec((tm, tn), lambda i, j: (i, 0))],
        out_shape=[jax.ShapeDtypeStruct((n, 2 * tn), BF16), jax.ShapeDtypeStruct((n, tn), F32)],
        compiler_params=_params(*sem), name="proj_diff",
    )(h16, w16, dcos, dsa, dsb, qkg, gmat)

    v32 = pl.pallas_call(
        _proj_val_kernel, grid=(n // tm, 1),
        in_specs=[h_spec, w_spec(OFF_VD)],
        out_specs=o_spec,
        out_shape=jax.ShapeDtypeStruct((n, tn), F32),
        compiler_params=_params(*sem), name="proj_val",
    )(h16, w16)
    return qk_r, act, qk_d, k32, v32


def _ret_kernel(*refs, c_len, n_chunks, has_state):
    if has_state:
        (q_ref, k_ref, v_ref, g_ref, decay_ref, xi_ref, zeta_ref, glen_ref, gain_ref, s0_ref,
         o_ref, s_ref) = refs
        s = s0_ref[0, 0]
    else:
        (q_ref, k_ref, v_ref, g_ref, decay_ref, xi_ref, zeta_ref, glen_ref, gain_ref,
         o_ref, s_ref) = refs
        s = jnp.zeros((RET_DK, RET_DV), F32)
    decay = decay_ref[0]
    for c in range(n_chunks):
        rows = slice(c * c_len, (c + 1) * c_len)
        q = q_ref[rows, :]
        k = k_ref[rows, :]
        v = v_ref[rows, :]
        sc = _dot_nt(q, k) * decay
        o = _dot(sc.astype(BF16), v) + _dot(q, s.astype(BF16)) * xi_ref[0]
        kz = (k.astype(F32) * zeta_ref[0]).astype(BF16)
        s = s * glen_ref[0] + _dot_tn(kz, v)
        ms = jnp.mean(o * o, axis=-1, keepdims=True)
        on = o * lax.rsqrt(ms + EPS) * gain_ref[...]
        o_ref[rows, :] = (on * g_ref[rows, :].astype(F32)).astype(BF16)
    s_ref[0, 0] = s


def _retention(qk_r, act, tables, gain, s0, *, batch, t_len, c_len):
    n = qk_r.shape[0]
    decay, xi, zeta, glen = tables
    has_state = s0 is not None
    in_specs = [
        pl.BlockSpec((t_len, RET_DK), lambda b, h: (b, h)),
        pl.BlockSpec((t_len, RET_DK), lambda b, h: (b, N_RET_HEADS + h)),
        pl.BlockSpec((t_len, RET_DV), lambda b, h: (b, h)),
        pl.BlockSpec((t_len, RET_DV), lambda b, h: (b, N_RET_HEADS + h)),
        pl.BlockSpec((1, c_len, c_len), lambda b, h: (h, 0, 0)),
        pl.BlockSpec((1, c_len, RET_DV), lambda b, h: (h, 0, 0)),
        pl.BlockSpec((1, c_len, RET_DK), lambda b, h: (h, 0, 0)),
        pl.BlockSpec((1, 1, RET_DV), lambda b, h: (h, 0, 0)),
        pl.BlockSpec((1, RET_DV), lambda b, h: (0, 0)),
    ]
    args = [qk_r, qk_r, act, act, decay, xi, zeta, glen, gain]
    if has_state:
        in_specs.append(pl.BlockSpec((1, 1, RET_DK, RET_DV), lambda b, h: (b, h, 0, 0)))
        args.append(s0)
    return pl.pallas_call(
        functools.partial(_ret_kernel, c_len=c_len, n_chunks=t_len // c_len, has_state=has_state),
        grid=(batch, N_RET_HEADS),
        in_specs=in_specs,
        out_specs=[
            pl.BlockSpec((t_len, RET_DV), lambda b, h: (b, h)),
            pl.BlockSpec((1, 1, RET_DK, RET_DV), lambda b, h: (b, h, 0, 0)),
        ],
        out_shape=[
            jax.ShapeDtypeStruct((n, RET_V), BF16),
            jax.ShapeDtypeStruct((batch, N_RET_HEADS, RET_DK, RET_DV), F32),
        ],
        compiler_params=_params("arbitrary", "arbitrary"),
        name="ret",
    )(*args)


def _attn_kernel(q_ref, k_ref, v_ref, kc_ref, vc_ref, lam_ref, gain_ref, o_ref, vt_scr, s_scr, p_scr, *, blocks,
                 tb, causal, n_valid_last, lam_init, head=None):
    cs = slice(None)
    n_cache = 0 if kc_ref is None else kc_ref.shape[1]
    lk = n_cache + k_ref.shape[0]
    tq = blocks[0][1]

    def k_tile(t):
        r0 = t * tb
        if r0 < n_cache:
            return kc_ref[head, r0:r0 + tb, :]
        return k_ref[r0 - n_cache:r0 - n_cache + tb, cs]

    if n_cache:
        vt_scr[:, 0:n_cache] = vc_ref[head]
    for t in range(n_cache // tb, lk // tb):
        r0 = t * tb
        v = v_ref[r0 - n_cache:r0 - n_cache + tb, cs].astype(F32)
        vt_scr[:, r0:r0 + tb] = v.T.astype(BF16)
    lp = lam_ref[...]
    lam = (jnp.exp(jnp.sum(lp[0:1] * lp[1:2], axis=1, keepdims=True))
           - jnp.exp(jnp.sum(lp[2:3] * lp[3:4], axis=1, keepdims=True)) + lam_init)
    kidx = lax.broadcasted_iota(jnp.int32, (tb, tq), 0)
    if causal:
        qidx = lax.broadcasted_iota(jnp.int32, (tb, tq), 1)
        visible = (kidx // CHUNK) <= (qidx // CHUNK)
    else:
        visible = kidx < n_valid_last
    bias = jnp.where(visible, 0.0, NEG_BIG).astype(F32)

    def score_tiles(bi):
        q0, _, nk = blocks[bi]
        slot = bi % 2
        q = q_ref[q0:q0 + tq, cs]
        lane = lax.broadcasted_iota(jnp.int32, q.shape, 1)
        qs = (jnp.where(lane < DIFF_HD, q, jnp.zeros_like(q)), jnp.where(lane >= DIFF_HD, q, jnp.zeros_like(q)))
        nt = nk // tb
        st = {"m": [None, None]}

        def tile(t):
            kt = k_tile(t)
            for c in range(2):
                s = _dot_nt(kt, qs[c])
                if t == nt - 1:
                    s = s + bias
                s_scr[slot, c, t] = s
                mt = jnp.max(s, axis=0, keepdims=True)
                st["m"][c] = mt if st["m"][c] is None else jnp.maximum(st["m"][c], mt)

        return [functools.partial(tile, t) for t in range(nt)], st

    def prob_tiles(bi, st1):
        q0, _, nk = blocks[bi]
        slot = bi % 2
        st = {"l": [None, None]}

        def tile(t):
            for c in range(2):
                p = jnp.exp2(s_scr[slot, c, t] - st1["m"][c])
                lt = jnp.sum(p, axis=0, keepdims=True)
                p_scr[slot, c, t * tb:(t + 1) * tb, :] = p.astype(BF16)
                st["l"][c] = lt if st["l"][c] is None else st["l"][c] + lt

        def finish():
            a0, a1 = (_dot(vt_scr[:, 0:nk], p_scr[slot, c, 0:nk, :]) for c in range(2))
            o = a0 * (1.0 / st["l"][0]) - (lam * (1.0 / st["l"][1])) * a1
            ms = jnp.mean(o * o, axis=0, keepdims=True)
            on = (o * lax.rsqrt(ms + EPS)).T
            o_ref[q0:q0 + tq, cs] = (on * gain_ref[...] * (1.0 - lam_init)).astype(BF16)

        return [functools.partial(tile, t) for t in range(nk // tb)], finish

    nb = len(blocks)
    t1, st1 = score_tiles(0)
    for f in t1:
        f()
    for bi in range(nb):
        t2, finish = prob_tiles(bi, st1)
        t1n, st1n = score_tiles(bi + 1) if bi + 1 < nb else ([], None)
        for t in range(max(len(t2), len(t1n))):
            if t < len(t1n):
                t1n[t]()
            if t < len(t2):
                t2[t]()
        finish()
        st1 = st1n


def _attention(q_arr, k_arr, v_arr, kc_arr, vc_arr, lam_p, gain, *, batch, lq, lk, qcol, kcol, vcol, blocks, tb,
               causal, n_valid_last, lam_init, layer=0):
    hd2 = 2 * DIFF_HD
    tq = blocks[0][1]
    has_cache = kc_arr is not None
    past = kc_arr.shape[2] if has_cache else 0
    lk_all = past + lk
    scratch = [pltpu.VMEM((hd2, lk_all), BF16),
               pltpu.VMEM((2, 2, lk_all // tb, tb, tq), F32),
               pltpu.VMEM((2, 2, lk_all, tq), BF16)]
    static = dict(blocks=blocks, tb=tb, causal=causal, n_valid_last=n_valid_last, lam_init=lam_init)
    small = lambda shape: pl.BlockSpec(shape, lambda *_: (0, 0))
    out_shape = jax.ShapeDtypeStruct((batch * lq, DIFF_W), BF16)

    if not has_cache:
        def body(q_ref, k_ref, v_ref, lam_ref, gain_ref, o_ref, vt_scr, s_scr, p_scr):
            _attn_kernel(q_ref, k_ref, v_ref, None, None, lam_ref, gain_ref, o_ref, vt_scr, s_scr, p_scr, **static)

        row = lambda col: (lambda b, h: (b, col + h))
        return pl.pallas_call(
            body,
            grid=(batch, N_DIFF_HEADS),
            in_specs=[pl.BlockSpec((lq, hd2), row(qcol)), pl.BlockSpec((lk, hd2), row(kcol)),
                      pl.BlockSpec((lk, hd2), row(vcol)), small((4, DIFF_HD)), small((1, hd2))],
            out_specs=pl.BlockSpec((lq, hd2), lambda b, h: (b, h)),
            out_shape=out_shape,
            scratch_shapes=scratch,
            compiler_params=_params("arbitrary", "arbitrary"),
            name="attn",
        )(q_arr, k_arr, v_arr, lam_p, gain)

    def body_cache(q_ref, k_ref, v_ref, kc_ref, vc_ref, lam_ref, gain_ref, o_ref, vt_scr, s_scr, p_scr, kh_scr,
                   vth_scr):
        h = pl.program_id(1)

        @pl.when(h == 0)
        def _():
            for r0 in range(0, past, tb):
                kh_scr[:, r0:r0 + tb, :] = pltpu.einshape("khd->hkd", kc_ref[r0:r0 + tb]).astype(BF16)
                vh = pltpu.einshape("khd->hkd", vc_ref[r0:r0 + tb])
                for hh in range(N_DIFF_HEADS):
                    vth_scr[hh, :, r0:r0 + tb] = vh[hh].T.astype(BF16)

        _attn_kernel(q_ref, k_ref, v_ref, kh_scr, vth_scr, lam_ref, gain_ref, o_ref, vt_scr, s_scr, p_scr,
                     head=h, **static)

    scratch += [pltpu.VMEM((N_DIFF_HEADS, past, hd2), BF16), pltpu.VMEM((N_DIFF_HEADS, hd2, past), BF16)]
    row = lambda col: (lambda b, h: (b, col + h))
    cache_spec = pl.BlockSpec((None, None, past, N_DIFF_HEADS, hd2), lambda b, h: (layer, b, 0, 0, 0))
    return pl.pallas_call(
        body_cache,
        grid=(batch, N_DIFF_HEADS),
        in_specs=[pl.BlockSpec((lq, hd2), row(qcol)), pl.BlockSpec((lk, hd2), row(kcol)),
                  pl.BlockSpec((lk, hd2), row(vcol)), cache_spec, cache_spec, small((4, DIFF_HD)),
                  small((1, hd2))],
        out_specs=pl.BlockSpec((lq, hd2), lambda b, h: (b, h)),
        out_shape=out_shape,
        scratch_shapes=scratch,
        compiler_params=_params("arbitrary", "arbitrary"),
        name="attn_cache",
    )(q_arr, k_arr, v_arr, kc_arr, vc_arr, lam_p, gain)


def _merge_kernel(ret_ref, dif_ref, sa_ref, sb_ref, x_ref, wa_ref, wb_ref, wo_ref, g2_ref, x1_ref, h2_ref):
    ya = _dot(ret_ref[...], wa_ref[...])
    yb = _dot(dif_ref[...], wb_ref[...])
    y = sa_ref[...].astype(F32) * ya + sb_ref[...].astype(F32) * yb
    x1 = x_ref[...] + _dot(y.astype(BF16), wo_ref[...])
    x1_ref[...] = x1
    h2_ref[...] = _rms(x1, g2_ref[...]).astype(BF16)


def _merge(ret16, dif16, act, x, wa16, wb16, wo16, g2, *, tm):
    n, d = x.shape
    full = lambda i: (0, 0)
    return pl.pallas_call(
        _merge_kernel,
        grid=(n // tm,),
        in_specs=[
            pl.BlockSpec((tm, RET_V), lambda i: (i, 0)),
            pl.BlockSpec((tm, DIFF_W), lambda i: (i, 0)),
            pl.BlockSpec((tm, d), lambda i: (i, 2)),
            pl.BlockSpec((tm, d), lambda i: (i, 3)),
            pl.BlockSpec((tm, d), lambda i: (i, 0)),
            pl.BlockSpec((RET_V, d), full),
            pl.BlockSpec((DIFF_W, d), full),
            pl.BlockSpec((d, d), full),
            pl.BlockSpec((1, d), full),
        ],
        out_specs=[pl.BlockSpec((tm, d), lambda i: (i, 0)), pl.BlockSpec((tm, d), lambda i: (i, 0))],
        out_shape=[jax.ShapeDtypeStruct((n, d), F32), jax.ShapeDtypeStruct((n, d), BF16)],
        compiler_params=_params("arbitrary"),
        name="merge",
    )(ret16, dif16, act, act, x, wa16, wb16, wo16, g2)


def _glu_kernel(*refs, tm, n_t, d_ff, cw, has_state, has_next):
    refs = list(refs)
    h_ref, x1_ref, wu_ref, wd_ref, cwt_ref, cb_ref = refs[:6]
    del refs[:6]
    st_ref = refs.pop(0) if has_state else None
    gn_ref = refs.pop(0) if has_next else None
    y_ref, cs_ref = refs[:2]
    del refs[:2]
    hn_ref = refs.pop(0) if has_next else None
    (carry,) = refs
    ti = pl.program_id(0) % n_t

    @pl.when(ti == 0)
    def _():
        carry[...] = jnp.zeros_like(carry)
        if has_state:
            carry[6:8, :] = st_ref[0]

    h = h_ref[...]
    row8 = lax.broadcasted_iota(jnp.int32, (8, cw), 0)

    def up(c):
        return [_dot(h, wu_ref[:, base:base + cw]) for base in (c * cw, d_ff + c * cw)]

    def rest(c, us, acc):
        conv = []
        for u, base in zip(us, (c * cw, d_ff + c * cw)):
            cols = slice(base, base + cw)
            c6 = jnp.broadcast_to(carry[6:7, cols], (8, cw))
            c7 = jnp.broadcast_to(carry[7:8, cols], (8, cw))
            carry[6:8, cols] = u[tm - 2:tm, :]
            r1 = pltpu.roll(u, 1, 0)
            r2 = pltpu.roll(u, 2, 0)
            top1 = jnp.where(row8 == 0, c7, r1[0:8])
            top2 = jnp.where(row8 == 0, c6, jnp.where(row8 == 1, c7, r2[0:8]))
            if tm > 8:
                sh1 = jnp.concatenate([top1, r1[8:]], axis=0)
                sh2 = jnp.concatenate([top2, r2[8:]], axis=0)
            else:
                sh1, sh2 = top1, top2
            r = cb_ref[:, cols] + cwt_ref[0:1, cols] * sh2
            r = r + cwt_ref[1:2, cols] * sh1
            r = r + cwt_ref[2:3, cols] * u
            conv.append(r)
        a, g = conv
        act = (g * _sigmoid(g) * a).astype(BF16)
        dn = _dot(act, wd_ref[c * cw:(c + 1) * cw, :])
        return dn if acc is None else acc + dn

    n_c = d_ff // cw
    acc = None
    us = up(0)
    for c in range(n_c):
        nxt = up(c + 1) if c + 1 < n_c else None
        acc = rest(c, us, acc)
        us = nxt
    y = x1_ref[...] + acc
    y_ref[...] = y
    if has_next:
        hn_ref[...] = _rms(y, gn_ref[...]).astype(BF16)

    @pl.when(ti == n_t - 1)
    def _():
        cs_ref[0] = carry[6:8, :]


def _glu(h2, x1, wu16, wd16, conv_w, conv_b, state, g_next, *, batch, t_len, tm):
    n, d = x1.shape
    d_ff = wd16.shape[0]
    n_t = t_len // tm
    has_state = state is not None
    has_next = g_next is not None
    full = lambda i: (0, 0)
    row = pl.BlockSpec((tm, d), lambda i: (i, 0))
    in_specs = [row, row, pl.BlockSpec((d, 2 * d_ff), full), pl.BlockSpec((d_ff, d), full),
                pl.BlockSpec((CONV_W, 2 * d_ff), full), pl.BlockSpec((1, 2 * d_ff), full)]
    args = [h2, x1, wu16, wd16, conv_w, conv_b]
    st_spec = pl.BlockSpec((1, CONV_W - 1, 2 * d_ff), lambda i: (i // n_t, 0, 0))
    if has_state:
        in_specs.append(st_spec)
        args.append(state)
    out_specs = [row, st_spec]
    out_shape = [jax.ShapeDtypeStruct((n, d), F32), jax.ShapeDtypeStruct((batch, CONV_W - 1, 2 * d_ff), F32)]
    if has_next:
        in_specs.append(pl.BlockSpec((1, d), full))
        args.append(g_next)
        out_specs.append(row)
        out_shape.append(jax.ShapeDtypeStruct((n, d), BF16))
    res = pl.pallas_call(
        functools.partial(_glu_kernel, tm=tm, n_t=n_t, d_ff=d_ff, cw=MXU_N, has_state=has_state,
                          has_next=has_next),
        grid=(n // tm,),
        in_specs=in_specs,
        out_specs=out_specs,
        out_shape=out_shape,
        scratch_shapes=[pltpu.VMEM((8, 2 * d_ff), F32)],
        compiler_params=_params("arbitrary"),
        name="glu",
    )(*args)
    return res if has_next else (*res, None)


def _rope_tables(pos, reps):
    posf = pos.astype(F32)[:, None]
    lane = np.arange(LANES)
    half = RET_DK // 2
    inv = RET_ROPE_THETA ** (-jnp.arange(half, dtype=F32) / half)
    ang = posf * inv[None, :]
    cos, sin = jnp.cos(ang), jnp.sin(ang)
    rcos = jnp.concatenate([cos, cos], axis=1)
    rsin = jnp.concatenate([-sin, sin], axis=1)
    half = DIFF_ROT // 2
    inv = ROPE_THETA ** (-jnp.arange(half, dtype=F32) / half)
    ang = posf * inv[None, :]
    cos, sin = jnp.cos(ang), jnp.sin(ang)
    grp = lane % DIFF_HD
    sel = jnp.asarray(grp % half)
    cos_l, sin_l = cos[:, sel], sin[:, sel]
    dcos = jnp.where(jnp.asarray(grp < DIFF_ROT)[None, :], cos_l, 1.0)
    dsa = jnp.where(jnp.asarray((grp >= half) & (grp < DIFF_ROT))[None, :], sin_l, 0.0)
    dsb = jnp.where(jnp.asarray(grp < half)[None, :], -sin_l, 0.0)
    return tuple(jnp.tile(t.astype(F32), (reps, 1)) for t in (rcos, rsin, dcos, dsa, dsb))


def _ret_tables(c_len):
    log_g = jnp.log1p(-jnp.exp2(-5.0 - jnp.arange(N_RET_HEADS, dtype=F32)))
    idx = jnp.arange(c_len, dtype=F32)
    rel = idx[:, None] - idx[None, :]
    decay = jnp.where(rel >= 0, jnp.exp(log_g[:, None, None] * jnp.maximum(rel, 0.0)), 0.0)
    xi = jnp.exp(log_g[:, None] * (idx[None, :] + 1.0))
    zeta = jnp.exp(log_g[:, None] * (c_len - 1.0 - idx[None, :]))
    glen = jnp.exp(log_g * c_len)
    return (decay.astype(F32),
            jnp.broadcast_to(xi[:, :, None], (N_RET_HEADS, c_len, RET_DV)).astype(F32),
            jnp.broadcast_to(zeta[:, :, None], (N_RET_HEADS, c_len, RET_DK)).astype(F32),
            jnp.broadcast_to(glen[:, None, None], (N_RET_HEADS, 1, RET_DV)).astype(F32))


def _pick_tile(n, target):
    t = min(n, target)
    while n % t:
        t //= 2
    return t


def kernel(x_prompt, x_sample, cache_k, cache_v, state_ret, state_conv, norm1_g, w_in, ret_norm_g, q_norm_g,
           k_norm_g, lambda_q1, lambda_k1, lambda_q2, lambda_k2, diff_norm_g, w_branch_a, w_branch_b, w_out,
           norm2_g, w_up, conv_w, conv_b, w_down):
    bp, tp, d = x_prompt.shape
    bs, ls, _ = x_sample.shape
    depth = w_in.shape[0]
    past = cache_k.shape[2]
    d_ff = w_down.shape[1]
    hd2 = 2 * DIFF_HD

    tm_p = _pick_tile(tp, 1024)
    tn = d
    c_p = _pick_tile(tp, 256)
    tq_p = _pick_tile(tp, 256)
    tm_glu = _pick_tile(tp, 512)
    ns = bs * ls
    tq_s = LANES
    assert past % LANES == 0 and ls <= tq_s
    blocks_p = tuple((i * tq_p, tq_p, (i + 1) * tq_p) for i in range(tp // tq_p))
    blocks_s = ((0, tq_s, past + tq_s),)

    pos_p = jnp.arange(tp, dtype=jnp.int32)
    pos_s = past + jnp.arange(ls, dtype=jnp.int32)
    tabs_p = _rope_tables(pos_p, 1)
    tabs_s = _rope_tables(pos_s, bs)
    rt_p = _ret_tables(c_p)
    rt_s = _ret_tables(ls)
    blk = np.arange(MXU_N) // DIFF_HD
    gmat = jnp.asarray(blk[:, None] == blk[None, :], dtype=BF16)

    yp = x_prompt.reshape(bp * tp, d)
    ys = x_sample.reshape(ns, d)
    hp = _norm(yp, norm1_g[0].reshape(1, d), tm=tm_p)
    hs = _norm(ys, norm1_g[0].reshape(1, d), tm=ns)
    outs = {k: [] for k in ("kp", "vp", "sp", "cp", "ks", "vs", "ss", "cs")}
    for l in range(depth):
        lam_init = 0.8 - 0.6 * math.exp(-0.3 * l)
        g2 = norm2_g[l].reshape(1, d)
        g_next = norm1_g[l + 1].reshape(1, d) if l + 1 < depth else None
        w16 = w_in[l].astype(BF16)
        wa16 = w_branch_a[l].astype(BF16)
        wb16 = w_branch_b[l].astype(BF16)
        wo16 = w_out[l].astype(BF16)
        wu16 = w_up[l].astype(BF16)
        wd16 = w_down[l].astype(BF16)
        reps = MXU_N // DIFF_HD
        qkg = jnp.stack([jnp.tile(q_norm_g[l], reps), jnp.tile(k_norm_g[l], reps)]).reshape(2, 1, MXU_N)
        rgain = ret_norm_g[l].reshape(1, RET_DV)
        dgain = diff_norm_g[l].reshape(1, hd2)
        lam_p = jnp.stack([lambda_q1[l], lambda_k1[l], lambda_q2[l], lambda_k2[l]])
        cw_l = conv_w[l]
        cb_l = conv_b[l].reshape(1, 2 * d_ff)

        qk_r, act, qk_d, k32, v32 = _proj(hp, w16, tabs_p, qkg, gmat, tm=tm_p, tn=tn)
        ret16, s_fin = _retention(qk_r, act, rt_p, rgain, None, batch=bp, t_len=tp, c_len=c_p)
        dif16 = _attention(qk_d, qk_d, v32, None, None, lam_p, dgain, batch=bp, lq=tp, lk=tp, qcol=0,
                           kcol=N_DIFF_HEADS, vcol=0, blocks=blocks_p, tb=tq_p, causal=True,
                           n_valid_last=0, lam_init=lam_init)
        x1, h2 = _merge(ret16, dif16, act, yp, wa16, wb16, wo16, g2, tm=tm_p)
        yp, cst, hp = _glu(h2, x1, wu16, wd16, cw_l, cb_l, None, g_next, batch=bp, t_len=tp, tm=tm_glu)
        outs["kp"].append(k32.reshape(bp, tp, N_DIFF_HEADS, hd2))
        outs["vp"].append(v32.reshape(bp, tp, N_DIFF_HEADS, hd2))
        outs["sp"].append(s_fin)
        outs["cp"].append(cst)

        qk_r, act, qk_d, k32, v32 = _proj(hs, w16, tabs_s, qkg, gmat, tm=ns, tn=tn)
        ret16, s_fin = _retention(qk_r, act, rt_s, rgain, state_ret[l], batch=bs, t_len=ls, c_len=ls)
        pad_rows = lambda a: jnp.pad(a.reshape(bs, ls, DIFF_W), ((0, 0), (0, tq_s - ls), (0, 0))).reshape(
            bs * tq_s, DIFF_W)
        dif_pad = _attention(pad_rows(qk_d[:, :DIFF_W]), pad_rows(qk_d[:, DIFF_W:]), pad_rows(v32),
                             cache_k, cache_v, lam_p, dgain, batch=bs, lq=tq_s, lk=tq_s, qcol=0, kcol=0,
                             vcol=0, blocks=blocks_s, tb=LANES, causal=False, n_valid_last=ls,
                             lam_init=lam_init, layer=l)
        dif16 = dif_pad.reshape(bs, tq_s, DIFF_W)[:, :ls].reshape(ns, DIFF_W)
        x1, h2 = _merge(ret16, dif16, act, ys, wa16, wb16, wo16, g2, tm=ns)
        ys, cst, hs = _glu(h2, x1, wu16, wd16, cw_l, cb_l, state_conv[l], g_next, batch=bs, t_len=ls, tm=ls)
        outs["ks"].append(k32.reshape(bs, ls, N_DIFF_HEADS, hd2))
        outs["vs"].append(v32.reshape(bs, ls, N_DIFF_HEADS, hd2))
        outs["ss"].append(s_fin)
        outs["cs"].append(cst)

    st = {k: jnp.stack(v) for k, v in outs.items()}
    return (yp.reshape(bp, tp, d), ys.reshape(bs, ls, d), st["kp"], st["vp"], st["sp"], st["cp"],
            st["ks"], st["vs"], st["ss"], st["cs"])
```
